```python
import jax, jax.numpy as jnp
from jax import lax
import numpy as np

D_MODEL = 1024
BATCH = 16
SEQ = 2048
DEPTH = 1

CHUNK = 64
Q_BLOCK = 128
EPS = 1e-6
D_FF = 2816

FOX_HEADS = 8
FOX_HEAD_DIM = 64
FOX_WIDTH = FOX_HEADS * FOX_HEAD_DIM

GLA_HEADS = 4
GLA_DK = 64
GLA_DV = 128
GLA_KEY_WIDTH = GLA_HEADS * GLA_DK
GLA_VAL_WIDTH = GLA_HEADS * GLA_DV
GLA_GATE_RANK = 16
GLA_GATE_TEMP = 16.0

MIX_WIDTH = FOX_WIDTH + GLA_VAL_WIDTH
IN_SPLITS = (FOX_WIDTH, FOX_WIDTH, FOX_WIDTH, FOX_HEADS,
             GLA_KEY_WIDTH, GLA_KEY_WIDTH, GLA_VAL_WIDTH, GLA_GATE_RANK, GLA_VAL_WIDTH)
IN_WIDTH = 3 * FOX_WIDTH + FOX_HEADS + 2 * GLA_KEY_WIDTH + GLA_VAL_WIDTH + GLA_GATE_RANK + GLA_VAL_WIDTH

kernel_name = "hymba_fox_gla_macaron_block"


def rms_norm(x, g):
    xf = x.astype(jnp.float32)
    y = xf * lax.rsqrt(jnp.mean(xf * xf, axis=-1, keepdims=True) + EPS)
    return (y * g.astype(jnp.float32)).astype(x.dtype)


def swiglu(h, w_gate, w_up, w_down):
    return (jax.nn.silu(h @ w_gate) * (h @ w_up)) @ w_down


def fox_attention(q, k, v, log_f):
    S, Dh = q.shape[1], q.shape[-1]
    F = jnp.transpose(jnp.cumsum(log_f, axis=1), (0, 2, 1))
    scale = Dh ** -0.5
    outs = []
    for blk in range(S // Q_BLOCK):
        q0, q1 = blk * Q_BLOCK, (blk + 1) * Q_BLOCK
        qb, kb, vb = q[:, q0:q1], k[:, :q1], v[:, :q1]
        s = jnp.einsum('bqhd,bkhd->bhqk', qb, kb).astype(jnp.float32) * scale
        s = s + F[:, :, q0:q1, None] - F[:, :, None, :q1]
        causal = jnp.arange(q1)[None, :] <= jnp.arange(q0, q1)[:, None]
        s = jnp.where(causal, s, -jnp.inf)
        p = jax.nn.softmax(s, axis=-1).astype(vb.dtype)
        outs.append(jnp.einsum('bhqk,bkhd->bqhd', p, vb))
    return jnp.concatenate(outs, axis=1)


def gla_chunk_causal(q, k, v, log_a):
    B, S, H, Dk = q.shape
    Dv = v.shape[-1]
    NC = S // CHUNK
    qc = q.reshape(B, NC, CHUNK, H, Dk).astype(jnp.float32) * (Dk ** -0.5)
    kc = k.reshape(B, NC, CHUNK, H, Dk).astype(jnp.float32)
    vc = v.reshape(B, NC, CHUNK, H, Dv).astype(jnp.float32)
    G = jnp.cumsum(log_a.reshape(B, NC, CHUNK, H, Dk), axis=2)
    G_tot = G[:, :, -1]
    k_dec = kc * jnp.exp(G_tot[:, :, None] - G)
    delta = jnp.einsum('bnchk,bnchv->bnhkv', k_dec, vc)
    chunk_decay = jnp.exp(G_tot)

    def step(state, inp):
        dec, d = inp
        state = dec[..., None] * state + d
        return state, state

    s0 = jnp.zeros((B, H, Dk, Dv), jnp.float32)
    _, states = lax.scan(step, s0, (jnp.moveaxis(chunk_decay, 1, 0), jnp.moveaxis(delta, 1, 0)))
    states = jnp.moveaxis(states, 0, 1)
    o = jnp.einsum('bnchk,bnhkv->bnchv', qc, states)
    return o.reshape(B, S, H, Dv).astype(v.dtype)


def setup_inputs(seed: int = 0) -> dict:
    key = jax.random.key(seed)
    ks = jax.random.split(key, 18)

    def w(k, shape, fan_in):
        return jax.random.normal(k, shape, jnp.float32) * fan_in ** -0.5

    def gain(k, shape):
        return 1.0 + 0.05 * jax.random.normal(k, shape, jnp.float32)

    L = DEPTH
    return {
        "x": jax.random.normal(ks[0], (BATCH, SEQ, D_MODEL), jnp.float32),
        "ffn1_norm": gain(ks[1], (L, D_MODEL)),
        "ffn1_w_gate": w(ks[2], (L, D_MODEL, D_FF), D_MODEL),
        "ffn1_w_up": w(ks[3], (L, D_MODEL, D_FF), D_MODEL),
        "ffn1_w_down": w(ks[4], (L, D_FF, D_MODEL), D_FF),
        "mix_norm": gain(ks[5], (L, D_MODEL)),
        "w_in": w(ks[6], (L, D_MODEL, IN_WIDTH), D_MODEL),
        "fox_forget_bias": jax.random.uniform(ks[7], (L, FOX_HEADS), jnp.float32, 1.0, 5.0),
        "gla_w_gate_up": w(ks[8], (L, GLA_GATE_RANK, GLA_KEY_WIDTH), GLA_GATE_RANK),
        "gla_gate_bias": 0.1 * jax.random.normal(ks[9], (L, GLA_KEY_WIDTH), jnp.float32),
        "gla_out_norm": gain(ks[10], (L, GLA_DV)),
        "w_out": w(ks[11], (L, MIX_WIDTH, D_MODEL), MIX_WIDTH),
        "ffn2_norm": gain(ks[12], (L, D_MODEL)),
        "ffn2_w_gate": w(ks[13], (L, D_MODEL, D_FF), D_MODEL),
        "ffn2_w_up": w(ks[14], (L, D_MODEL, D_FF), D_MODEL),
        "ffn2_w_down": w(ks[15], (L, D_FF, D_MODEL), D_FF),
        "final_norm": gain(ks[16], (D_MODEL,)),
    }


def reference(x, ffn1_norm, ffn1_w_gate, ffn1_w_up, ffn1_w_down, mix_norm, w_in,
              fox_forget_bias, gla_w_gate_up, gla_gate_bias, gla_out_norm, w_out,
              ffn2_norm, ffn2_w_gate, ffn2_w_up, ffn2_w_down, final_norm):
    B, S, _ = x.shape
    split_idx = [int(i) for i in np.cumsum(IN_SPLITS)[:-1]]
    for l in range(DEPTH):
        h = rms_norm(x, ffn1_norm[l])
        x = x + 0.5 * swiglu(h, ffn1_w_gate[l], ffn1_w_up[l], ffn1_w_down[l])

        h = rms_norm(x, mix_norm[l])
        proj = h @ w_in[l]
        fq, fk, fv, ff, gq, gk, gv, g_low, g_out = jnp.split(proj, split_idx, axis=-1)

        log_f = jax.nn.log_sigmoid((ff + fox_forget_bias[l]).astype(jnp.float32))
        fox = fox_attention(fq.reshape(B, S, FOX_HEADS, FOX_HEAD_DIM),
                            fk.reshape(B, S, FOX_HEADS, FOX_HEAD_DIM),
                            fv.reshape(B, S, FOX_HEADS, FOX_HEAD_DIM), log_f)
        fox = fox.reshape(B, S, FOX_WIDTH)

        log_a = jax.nn.log_sigmoid((g_low @ gla_w_gate_up[l] + gla_gate_bias[l]).astype(jnp.float32)) / GLA_GATE_TEMP
        gla = gla_chunk_causal(gq.reshape(B, S, GLA_HEADS, GLA_DK),
                               gk.reshape(B, S, GLA_HEADS, GLA_DK),
                               gv.reshape(B, S, GLA_HEADS, GLA_DV),
                               log_a.reshape(B, S, GLA_HEADS, GLA_DK))
        gla = rms_norm(gla, gla_out_norm[l]) * jax.nn.silu(g_out.reshape(B, S, GLA_HEADS, GLA_DV))
        gla = gla.reshape(B, S, GLA_VAL_WIDTH)

        x = x + jnp.concatenate([fox, gla], axis=-1) @ w_out[l]

        h = rms_norm(x, ffn2_norm[l])
        x = x + 0.5 * swiglu(h, ffn2_w_gate[l], ffn2_w_up[l], ffn2_w_down[l])
    return rms_norm(x, final_norm)
```

```python
import functools

import jax
import jax.numpy as jnp
from jax import lax
from jax.experimental import pallas as pl
from jax.experimental.pallas import tpu as pltpu

F32 = jnp.float32
BF16 = jnp.bfloat16

D_MODEL = 1024
D_FF = 2816
EPS = 1e-6
CHUNK = 64

FOX_HEADS = 8
FOX_DH = 64
FOX_WIDTH = FOX_HEADS * FOX_DH
GLA_HEADS = 4
GLA_DK = 64
GLA_DV = 128
GLA_KW = GLA_HEADS * GLA_DK
GLA_VW = GLA_HEADS * GLA_DV
GLA_RANK = 16
GLA_TEMP = 16.0

LANES = 128
SMALL_W = LANES
_C_FQ = 0
_C_FK = _C_FQ + FOX_WIDTH
_C_FV = _C_FK + FOX_WIDTH
_C_GQ = _C_FV + FOX_WIDTH
_C_GK = _C_GQ + GLA_KW
_C_GV = _C_GK + GLA_KW
_C_GO = _C_GV + GLA_VW
_C_SM = _C_GO + GLA_VW
PROJ_W = _C_SM + SMALL_W

TM_FFN = 512
TM_PROJ = 512
TQ = 256
TG = 512
VMEM_LIMIT = 56 * 1024 * 1024


def _rms(x, g):
    return x * lax.rsqrt(jnp.mean(x * x, axis=-1, keepdims=True) + EPS) * g


def _log_sigmoid(z):
    return jnp.minimum(z, 0.0) - jnp.log1p(jnp.exp(-jnp.abs(z)))


def _swiglu(h, wg_ref, wu_ref, wd_ref):
    gate = jnp.dot(h, wg_ref[...], preferred_element_type=F32)
    up = jnp.dot(h, wu_ref[...], preferred_element_type=F32)
    act = (gate * jax.nn.sigmoid(gate) * up).astype(BF16)
    return jnp.dot(act, wd_ref[...], preferred_element_type=F32)


def _split3(x):
    hi = x.astype(BF16)
    r1 = x - hi.astype(F32)
    mid = r1.astype(BF16)
    lo = (r1 - mid.astype(F32)).astype(BF16)
    return hi, mid, lo


def _dot3(sel, x):
    hi, mid, lo = _split3(x)
    return (jnp.dot(sel, hi, preferred_element_type=F32)
            + jnp.dot(sel, mid, preferred_element_type=F32)
            + jnp.dot(sel, lo, preferred_element_type=F32))


def _const_spec(shape):
    n = len(shape)
    return pl.BlockSpec(shape, lambda *_: (0,) * n, pipeline_mode=pl.Buffered(1))


def _ffn1_kernel(x_ref, g_ref, wg_ref, wu_ref, wd_ref, o_ref):
    x = x_ref[...]
    h = _rms(x, g_ref[...]).astype(BF16)
    o_ref[...] = x + 0.5 * _swiglu(h, wg_ref, wu_ref, wd_ref)


def _ffn1(x, g, wg, wu, wd):
    n = x.shape[0]
    row = pl.BlockSpec((TM_FFN, D_MODEL), lambda i: (i, 0))
    return pl.pallas_call(
        _ffn1_kernel,
        out_shape=jax.ShapeDtypeStruct((n, D_MODEL), F32),
        grid=(n // TM_FFN,),
        in_specs=[row, _const_spec((1, D_MODEL)), _const_spec((D_MODEL, D_FF)),
                  _const_spec((D_MODEL, D_FF)), _const_spec((D_FF, D_MODEL))],
        out_specs=row,
        compiler_params=pltpu.CompilerParams(
            dimension_semantics=("arbitrary",), vmem_limit_bytes=VMEM_LIMIT),
        name="ffn1",
    )(x, g, wg, wu, wd)


def _proj_kernel(x_ref, g_ref, w_ref, fq_ref, fk_ref, fv_ref, gq_ref, gk_ref, gv_ref,
                 go_ref, sm_ref):
    h = _rms(x_ref[...], g_ref[...]).astype(BF16)

    def proj(c0, width):
        return jnp.dot(h, w_ref[:, c0:c0 + width], preferred_element_type=F32)

    fq_ref[...] = proj(_C_FQ, FOX_WIDTH).astype(BF16)
    fk_ref[...] = proj(_C_FK, FOX_WIDTH).astype(BF16)
    fv_ref[...] = proj(_C_FV, FOX_WIDTH).astype(BF16)
    gq_ref[...] = proj(_C_GQ, GLA_KW).astype(BF16)
    gk_ref[...] = proj(_C_GK, GLA_KW)
    gv_ref[...] = proj(_C_GV, GLA_VW).astype(BF16)
    go_ref[...] = proj(_C_GO, GLA_VW)
    sm_ref[...] = proj(_C_SM, SMALL_W)


def _in_proj(x1, g, w):
    n = x1.shape[0]

    def rows(width):
        return pl.BlockSpec((TM_PROJ, width), lambda i: (i, 0))

    widths = (FOX_WIDTH, FOX_WIDTH, FOX_WIDTH, GLA_KW, GLA_KW, GLA_VW, GLA_VW, SMALL_W)
    dtypes = (BF16, BF16, BF16, BF16, F32, BF16, F32, F32)
    return pl.pallas_call(
        _proj_kernel,
        out_shape=[jax.ShapeDtypeStruct((n, wd), dt) for wd, dt in zip(widths, dtypes)],
        grid=(n // TM_PROJ,),
        in_specs=[rows(D_MODEL), _const_spec((1, D_MODEL)), _const_spec((D_MODEL, PROJ_W))],
        out_specs=[rows(wd) for wd in widths],
        compiler_params=pltpu.CompilerParams(
            dimension_semantics=("arbitrary",), vmem_limit_bytes=VMEM_LIMIT),
        name="in_proj",
    )(x1, g, w)


_DECAY_BLK = 256


def _decay_kernel(sm_ref, b_ref, f_ref, ft_ref):
    seq = sm_ref.shape[1]
    r = lax.broadcasted_iota(jnp.int32, (_DECAY_BLK, _DECAY_BLK), 0)
    c = lax.broadcasted_iota(jnp.int32, (_DECAY_BLK, _DECAY_BLK), 1)
    tri = (r >= c).astype(BF16)
    lane = lax.broadcasted_iota(jnp.int32, (_DECAY_BLK, SMALL_W), 1)
    carry = jnp.zeros((1, SMALL_W), F32)
    for blk in range(seq // _DECAY_BLK):
        rows = slice(blk * _DECAY_BLK, (blk + 1) * _DECAY_BLK)
        log_f = _log_sigmoid(sm_ref[0, rows, :] + b_ref[...])
        log_f = jnp.where(lane < FOX_HEADS, log_f, 0.0)
        cum = _dot3(tri, log_f) + carry
        f_ref[0, rows, :] = cum
        carry = cum[_DECAY_BLK - 1:_DECAY_BLK, :]
    ft_ref[0] = f_ref[0].T[:FOX_HEADS, :]


def _fox_decay(small, bias):
    b, s, _ = small.shape
    return pl.pallas_call(
        _decay_kernel,
        out_shape=[jax.ShapeDtypeStruct((b, s, SMALL_W), F32),
                   jax.ShapeDtypeStruct((b, FOX_HEADS, s), F32)],
        grid=(b,),
        in_specs=[pl.BlockSpec((1, s, SMALL_W), lambda i: (i, 0, 0)),
                  pl.BlockSpec((1, SMALL_W), lambda i: (0, 0))],
        out_specs=[pl.BlockSpec((1, s, SMALL_W), lambda i: (i, 0, 0)),
                   pl.BlockSpec((1, FOX_HEADS, s), lambda i: (i, 0, 0))],
        compiler_params=pltpu.CompilerParams(dimension_semantics=("arbitrary",)),
        name="fox_decay",
    )(small, bias)


_NEG_INIT = -1e30
_NT = (((1,), (1,)), ((), ()))


def _attn_kernel(q_ref, k_ref, v_ref, f_ref, ft_ref, o_ref):
    hp = pl.program_id(1)
    qi = pl.program_id(2)
    q = q_ref[0]
    f_blk = f_ref[0]
    lane = lax.broadcasted_iota(jnp.int32, (TQ, LANES), 1)
    row = lax.broadcasted_iota(jnp.int32, (TQ, TQ), 0)
    col = lax.broadcasted_iota(jnp.int32, (TQ, TQ), 1)
    causal = col <= row

    q_heads, f_rows = [], []
    for hh in range(2):
        in_head = (lane >= hh * FOX_DH) & (lane < (hh + 1) * FOX_DH)
        q_heads.append(jnp.where(in_head, q, jnp.zeros_like(q)))
        f_rows.append(jnp.sum(jnp.where(lane == 2 * hp + hh, f_blk, 0.0), axis=-1, keepdims=True))

    def step(j, carry, diagonal):
        k0 = pl.multiple_of(j * TQ, TQ)
        kb = k_ref[0, pl.ds(k0, TQ), :]
        vb = v_ref[0, pl.ds(k0, TQ), :]
        out = []
        for hh in range(2):
            m, l, acc = carry[hh]
            s = lax.dot_general(q_heads[hh], kb, _NT, preferred_element_type=F32)
            s = s + f_rows[hh] - ft_ref[0, 0, hh:hh + 1, pl.ds(k0, TQ)]
            if diagonal:
                s = jnp.where(causal, s, -jnp.inf)
            m_new = jnp.maximum(m, jnp.max(s, axis=-1, keepdims=True))
            alpha = jnp.exp(m - m_new)
            p = jnp.exp(s - m_new)
            l = alpha * l + jnp.sum(p, axis=-1, keepdims=True)
            acc = alpha * acc + jnp.dot(p.astype(BF16), vb, preferred_element_type=F32)
            out.append((m_new, l, acc))
        return tuple(out)

    init = tuple((jnp.full((TQ, 1), _NEG_INIT, F32), jnp.zeros((TQ, 1), F32),
                  jnp.zeros((TQ, LANES), F32)) for _ in range(2))
    carry = lax.fori_loop(0, qi, lambda j, c: step(j, c, False), init)
    carry = step(qi, carry, True)
    o0 = carry[0][2] / carry[0][1]
    o1 = carry[1][2] / carry[1][1]
    o_ref[0] = jnp.where(lane < FOX_DH, o0, o1).astype(o_ref.dtype)


def _fox_attn(fq, fk, fv, f_col, f_row):
    b, s, _ = fq.shape
    pairs = FOX_HEADS // 2
    f_row = f_row.reshape(b, pairs, 2, s)
    return pl.pallas_call(
        _attn_kernel,
        out_shape=jax.ShapeDtypeStruct((b, s, FOX_WIDTH), BF16),
        grid=(b, pairs, s // TQ),
        in_specs=[pl.BlockSpec((1, TQ, LANES), lambda i, h, q: (i, q, h)),
                  pl.BlockSpec((1, s, LANES), lambda i, h, q: (i, 0, h)),
                  pl.BlockSpec((1, s, LANES), lambda i, h, q: (i, 0, h)),
                  pl.BlockSpec((1, TQ, SMALL_W), lambda i, h, q: (i, q, 0)),
                  pl.BlockSpec((1, 1, 2, s), lambda i, h, q: (i, h, 0, 0))],
        out_specs=pl.BlockSpec((1, TQ, LANES), lambda i, h, q: (i, q, h)),
        compiler_params=pltpu.CompilerParams(
            dimension_semantics=("arbitrary", "arbitrary", "arbitrary")),
        name="fox_attn",
    )(fq, fk, fv, f_col, f_row)


_TN = (((0,), (0,)), ((), ()))
_N_CHUNK = TG // CHUNK
_CHUNK_SHIFT = CHUNK.bit_length() - 1
_DK_SHIFT = GLA_DK.bit_length() - 1


def _gla_kernel(gq_ref, gk_ref, gv_ref, go_ref, sm_ref, w2_ref, b2_ref, gn_ref, o_ref, st_ref):
    @pl.when(pl.program_id(1) == 0)
    def _():
        st_ref[...] = jnp.zeros_like(st_ref)

    z = jnp.dot(sm_ref[0].astype(BF16), w2_ref[...], preferred_element_type=F32) + b2_ref[...]
    log_a = _log_sigmoid(z) / GLA_TEMP

    r = lax.broadcasted_iota(jnp.int32, (TG, TG), 0)
    c = lax.broadcasted_iota(jnp.int32, (TG, TG), 1)
    same_chunk = (r >> _CHUNK_SHIFT) == (c >> _CHUNK_SHIFT)
    tri = (same_chunk & (r >= c)).astype(BF16)
    ones = same_chunk.astype(BF16)
    hi, mid, lo = _split3(log_a)

    def sel_dot(sel):
        return (jnp.dot(sel, hi, preferred_element_type=F32)
                + jnp.dot(sel, mid, preferred_element_type=F32)
                + jnp.dot(sel, lo, preferred_element_type=F32))

    g_cum = sel_dot(tri)
    g_tot = sel_dot(ones)
    k_dec = (gk_ref[0] * jnp.exp(g_tot - g_cum)).astype(BF16)
    chunk_decay = jnp.exp(g_tot)

    q = gq_ref[0]
    lane_k = lax.broadcasted_iota(jnp.int32, (CHUNK, GLA_KW), 1) >> _DK_SHIFT
    lane_s = lax.broadcasted_iota(jnp.int32, (GLA_DV, GLA_KW), 1) >> _DK_SHIFT
    g_norm = gn_ref[...]

    state = st_ref[...]
    for n in range(_N_CHUNK):
        rows = slice(n * CHUNK, (n + 1) * CHUNK)
        full = lax.dot_general(gv_ref[0, rows, :], k_dec[rows], _TN, preferred_element_type=F32)
        delta = jnp.zeros((GLA_DV, GLA_KW), F32)
        for h in range(GLA_HEADS):
            delta = delta + jnp.where(lane_s == h, full[h * GLA_DV:(h + 1) * GLA_DV, :], 0.0)
        state = chunk_decay[n * CHUNK:n * CHUNK + 1, :] * state + delta
        qc = q[rows]
        q_stack = jnp.concatenate(
            [jnp.where(lane_k == h, qc, jnp.zeros_like(qc)) for h in range(GLA_HEADS)], axis=0)
        o_all = lax.dot_general(q_stack, state.astype(BF16), _NT, preferred_element_type=F32)
        for h in range(GLA_HEADS):
            o = o_all[h * CHUNK:(h + 1) * CHUNK, :]
            o = _rms(o, g_norm)
            gate = go_ref[0, rows, h * GLA_DV:(h + 1) * GLA_DV]
            o_ref[0, rows, h * GLA_DV:(h + 1) * GLA_DV] = (
                o * (gate * jax.nn.sigmoid(gate))).astype(o_ref.dtype)
    st_ref[...] = state


def _gla(gq, gk, gv, go, small, w2, b2, gn):
    b, s, _ = gq.shape

    def rows(width):
        return pl.BlockSpec((1, TG, width), lambda i, t: (i, t, 0))

    def const(shape):
        return pl.BlockSpec(shape, lambda i, t: (0, 0))

    return pl.pallas_call(
        _gla_kernel,
        out_shape=jax.ShapeDtypeStruct((b, s, GLA_VW), BF16),
        grid=(b, s // TG),
        in_specs=[rows(GLA_KW), rows(GLA_KW), rows(GLA_VW), rows(GLA_VW), rows(SMALL_W),
                  const((SMALL_W, GLA_KW)), const((1, GLA_KW)), const((1, GLA_DV))],
        out_specs=rows(GLA_VW),
        scratch_shapes=[pltpu.VMEM((GLA_DV, GLA_KW), F32)],
        compiler_params=pltpu.CompilerParams(dimension_semantics=("arbitrary", "arbitrary")),
        name="gla",
    )(gq, gk, gv, go, small, w2, b2, gn)


def _out_kernel(x_ref, fox_ref, gla_ref, wo_ref, g_ref, wg_ref, wu_ref, wd_ref, gf_ref, o_ref):
    x = x_ref[...]
    x = x + jnp.dot(fox_ref[...], wo_ref[:FOX_WIDTH, :], preferred_element_type=F32)
    x = x + jnp.dot(gla_ref[...], wo_ref[FOX_WIDTH:, :], preferred_element_type=F32)
    h = _rms(x, g_ref[...]).astype(BF16)
    x = x + 0.5 * _swiglu(h, wg_ref, wu_ref, wd_ref)
    o_ref[...] = _rms(x, gf_ref[...])


def _out_ffn2(x1, fox, gla, wo, g, wg, wu, wd, gf):
    n = x1.shape[0]

    def rows(width):
        return pl.BlockSpec((TM_FFN, width), lambda i: (i, 0))

    return pl.pallas_call(
        _out_kernel,
        out_shape=jax.ShapeDtypeStruct((n, D_MODEL), F32),
        grid=(n // TM_FFN,),
        in_specs=[rows(D_MODEL), rows(FOX_WIDTH), rows(GLA_VW),
                  _const_spec((FOX_WIDTH + GLA_VW, D_MODEL)), _const_spec((1, D_MODEL)),
                  _const_spec((D_MODEL, D_FF)), _const_spec((D_MODEL, D_FF)),
                  _const_spec((D_FF, D_MODEL)), _const_spec((1, D_MODEL))],
        out_specs=rows(D_MODEL),
        compiler_params=pltpu.CompilerParams(
            dimension_semantics=("arbitrary",), vmem_limit_bytes=VMEM_LIMIT),
        name="out_ffn2",
    )(x1, fox, gla, wo, g, wg, wu, wd, gf)


def _pack_w_in(w_in):
    widths = (FOX_WIDTH, FOX_WIDTH, FOX_WIDTH, FOX_HEADS, GLA_KW, GLA_KW, GLA_VW, GLA_RANK, GLA_VW)
    offs = [0]
    for wd in widths:
        offs.append(offs[-1] + wd)
    fq, fk, fv, ff, gq, gk, gv, glow, gout = (w_in[:, offs[i]:offs[i + 1]] for i in range(9))
    pad = jnp.zeros((D_MODEL, SMALL_W - FOX_HEADS - GLA_RANK), w_in.dtype)
    packed = jnp.concatenate(
        [fq * (FOX_DH ** -0.5), fk, fv, gq * (GLA_DK ** -0.5), gk, gv, gout, ff, glow, pad], axis=1)
    return packed.astype(BF16)


def kernel(x, ffn1_norm, ffn1_w_gate, ffn1_w_up, ffn1_w_down, mix_norm, w_in, fox_forget_bias,
           gla_w_gate_up, gla_gate_bias, gla_out_norm, w_out, ffn2_norm, ffn2_w_gate, ffn2_w_up,
           ffn2_w_down, final_norm):
    b, s, d = x.shape
    n = b * s
    assert ffn1_norm.shape[0] == 1, "the final norm is fused into the single layer's last call"
    x = x.reshape(n, d)
    for l in range(1):
        x1 = _ffn1(x, ffn1_norm[l].reshape(1, d), ffn1_w_gate[l].astype(BF16),
                   ffn1_w_up[l].astype(BF16), ffn1_w_down[l].astype(BF16))

        fq, fk, fv, gq, gk, gv, go, small = _in_proj(
            x1, mix_norm[l].reshape(1, d), _pack_w_in(w_in[l]))

        bias = jnp.zeros((1, SMALL_W), F32).at[0, :FOX_HEADS].set(fox_forget_bias[l])
        f_col, f_row = _fox_decay(small.reshape(b, s, SMALL_W), bias)
        fox = _fox_attn(fq.reshape(b, s, -1), fk.reshape(b, s, -1), fv.reshape(b, s, -1),
                        f_col, f_row)

        w2 = jnp.zeros((SMALL_W, GLA_KW), F32).at[FOX_HEADS:FOX_HEADS + GLA_RANK].set(
            gla_w_gate_up[l]).astype(BF16)
        gla = _gla(gq.reshape(b, s, -1), gk.reshape(b, s, -1), gv.reshape(b, s, -1),
                   go.reshape(b, s, -1), small.reshape(b, s, SMALL_W), w2,
                   gla_gate_bias[l].reshape(1, GLA_KW), gla_out_norm[l].reshape(1, GLA_DV))

        x = _out_ffn2(x1, fox.reshape(n, -1), gla.reshape(n, -1), w_out[l].astype(BF16),
                      ffn2_norm[l].reshape(1, d), ffn2_w_gate[l].astype(BF16),
                      ffn2_w_up[l].astype(BF16), ffn2_w_down[l].astype(BF16),
                      final_norm.reshape(1, d))
    return x.reshape(b, s, d)
```

```python
import functools

import jax
import jax.numpy as jnp
from jax import lax
from jax.experimental import pallas as pl
from jax.experimental.pallas import tpu as pltpu

F32 = jnp.float32
BF16 = jnp.bfloat16

D_MODEL = 1024
D_FF = 2816
EPS = 1e-6
CHUNK = 64

FOX_HEADS = 8
FOX_DH = 64
FOX_WIDTH = FOX_HEADS * FOX_DH
GLA_HEADS = 4
GLA_DK = 64
GLA_DV = 128
GLA_KW = GLA_HEADS * GLA_DK
GLA_VW = GLA_HEADS * GLA_DV
GLA_RANK = 16
GLA_TEMP = 16.0

LANES = 128
SMALL_W = LANES
_C_FQ = 0
_C_FK = _C_FQ + FOX_WIDTH
_C_FV = _C_FK + FOX_WIDTH
_C_GQ = _C_FV + FOX_WIDTH
_C_GK = _C_GQ + GLA_KW
_C_GV = _C_GK + GLA_KW
_C_GO = _C_GV + GLA_VW
_C_SM = _C_GO + GLA_VW
PROJ_W = _C_SM + SMALL_W

TM_FFN = 512
TM_PROJ = 512
TQ = 256
TG = 512
VMEM_LIMIT = 56 * 1024 * 1024


def _rms(x, g):
    return x * lax.rsqrt(jnp.mean(x * x, axis=-1, keepdims=True) + EPS) * g


def _log_sigmoid(z):
    return jnp.minimum(z, 0.0) - jnp.log1p(jnp.exp(-jnp.abs(z)))


def _swiglu(h, wg_ref, wu_ref, wd_ref):
    gate = jnp.dot(h, wg_ref[...], preferred_element_type=F32)
    up = jnp.dot(h, wu_ref[...], preferred_element_type=F32)
    act = (gate * jax.nn.sigmoid(gate) * up).astype(BF16)
    return jnp.dot(act, wd_ref[...], preferred_element_type=F32)


def _split3(x):
    hi = x.astype(BF16)
    r1 = x - hi.astype(F32)
    mid = r1.astype(BF16)
    lo = (r1 - mid.astype(F32)).astype(BF16)
    return hi, mid, lo


def _dot3(sel, x):
    hi, mid, lo = _split3(x)
    return (jnp.dot(sel, hi, preferred_element_type=F32)
            + jnp.dot(sel, mid, preferred_element_type=F32)
            + jnp.dot(sel, lo, preferred_element_type=F32))


def _const_spec(shape):
    n = len(shape)
    return pl.BlockSpec(shape, lambda *_: (0,) * n, pipeline_mode=pl.Buffered(1))


def _ffn1_kernel(x_ref, g_ref, wg_ref, wu_ref, wd_ref, o_ref):
    x = x_ref[...]
    h = _rms(x, g_ref[...]).astype(BF16)
    o_ref[...] = x + 0.5 * _swiglu(h, wg_ref, wu_ref, wd_ref)


def _ffn1(x, g, wg, wu, wd):
    n = x.shape[0]
    row = pl.BlockSpec((TM_FFN, D_MODEL), lambda i: (i, 0))
    return pl.pallas_call(
        _ffn1_kernel,
        out_shape=jax.ShapeDtypeStruct((n, D_MODEL), F32),
        grid=(n // TM_FFN,),
        in_specs=[row, _const_spec((1, D_MODEL)), _const_spec((D_MODEL, D_FF)),
                  _const_spec((D_MODEL, D_FF)), _const_spec((D_FF, D_MODEL))],
        out_specs=row,
        compiler_params=pltpu.CompilerParams(
            dimension_semantics=("arbitrary",), vmem_limit_bytes=VMEM_LIMIT),
        name="ffn1",
    )(x, g, wg, wu, wd)


def _proj_kernel(x_ref, g_ref, w_ref, fq_ref, fk_ref, fv_ref, gq_ref, gk_ref, gv_ref,
                 go_ref, sm_ref):
    h = _rms(x_ref[...], g_ref[...]).astype(BF16)

    def proj(c0, width):
        return jnp.dot(h, w_ref[:, c0:c0 + width], preferred_element_type=F32)

    fq_ref[...] = proj(_C_FQ, FOX_WIDTH).astype(BF16)
    fk_ref[...] = proj(_C_FK, FOX_WIDTH).astype(BF16)
    fv_ref[...] = proj(_C_FV, FOX_WIDTH).astype(BF16)
    gq_ref[...] = proj(_C_GQ, GLA_KW).astype(BF16)
    gk_ref[...] = proj(_C_GK, GLA_KW)
    gv_ref[...] = proj(_C_GV, GLA_VW).astype(BF16)
    go_ref[...] = proj(_C_GO, GLA_VW)
    sm_ref[...] = proj(_C_SM, SMALL_W)


def _in_proj(x1, g, w):
    n = x1.shape[0]

    def rows(width):
        return pl.BlockSpec((TM_PROJ, width), lambda i: (i, 0))

    widths = (FOX_WIDTH, FOX_WIDTH, FOX_WIDTH, GLA_KW, GLA_KW, GLA_VW, GLA_VW, SMALL_W)
    dtypes = (BF16, BF16, BF16, BF16, F32, BF16, F32, F32)
    return pl.pallas_call(
        _proj_kernel,
        out_shape=[jax.ShapeDtypeStruct((n, wd), dt) for wd, dt in zip(widths, dtypes)],
        grid=(n // TM_PROJ,),
        in_specs=[rows(D_MODEL), _const_spec((1, D_MODEL)), _const_spec((D_MODEL, PROJ_W))],
        out_specs=[rows(wd) for wd in widths],
        compiler_params=pltpu.CompilerParams(
            dimension_semantics=("arbitrary",), vmem_limit_bytes=VMEM_LIMIT),
        name="in_proj",
    )(x1, g, w)


_DECAY_BLK = 256


def _decay_kernel(sm_ref, b_ref, f_ref, ft_ref):
    seq = sm_ref.shape[1]
    r = lax.broadcasted_iota(jnp.int32, (_DECAY_BLK, _DECAY_BLK), 0)
    c = lax.broadcasted_iota(jnp.int32, (_DECAY_BLK, _DECAY_BLK), 1)
    tri = (r >= c).astype(BF16)
    lane = lax.broadcasted_iota(jnp.int32, (_DECAY_BLK, SMALL_W), 1)
    carry = jnp.zeros((1, SMALL_W), F32)
    for blk in range(seq // _DECAY_BLK):
        rows = slice(blk * _DECAY_BLK, (blk + 1) * _DECAY_BLK)
        log_f = _log_sigmoid(sm_ref[0, rows, :] + b_ref[...])
        log_f = jnp.where(lane < FOX_HEADS, log_f, 0.0)
        cum = _dot3(tri, log_f) + carry
        f_ref[0, rows, :] = cum
        carry = cum[_DECAY_BLK - 1:_DECAY_BLK, :]
    ft_ref[0] = f_ref[0].T[:FOX_HEADS, :]


def _fox_decay(small, bias):
    b, s, _ = small.shape
    return pl.pallas_call(
        _decay_kernel,
        out_shape=[jax.ShapeDtypeStruct((b, s, SMALL_W), F32),
                   jax.ShapeDtypeStruct((b, FOX_HEADS, s), F32)],
        grid=(b,),
        in_specs=[pl.BlockSpec((1, s, SMALL_W), lambda i: (i, 0, 0)),
                  pl.BlockSpec((1, SMALL_W), lambda i: (0, 0))],
        out_specs=[pl.BlockSpec((1, s, SMALL_W), lambda i: (i, 0, 0)),
                   pl.BlockSpec((1, FOX_HEADS, s), lambda i: (i, 0, 0))],
        compiler_params=pltpu.CompilerParams(dimension_semantics=("arbitrary",)),
        name="fox_decay",
    )(small, bias)


_NT = (((1,), (1,)), ((), ()))


def _attn_kernel(q_ref, k_ref, v_ref, f_ref, ft_ref, o_ref):
    hp = pl.program_id(1)
    seq = q_ref.shape[1]
    lane = lax.broadcasted_iota(jnp.int32, (TQ, LANES), 1)
    row = lax.broadcasted_iota(jnp.int32, (TQ, TQ), 0)
    col = lax.broadcasted_iota(jnp.int32, (TQ, TQ), 1)
    causal = col <= row

    for qi in range(seq // TQ):
        q0 = qi * TQ
        klen = q0 + TQ
        q = q_ref[0, q0:klen, :]
        f_blk = f_ref[0, q0:klen, :]
        outs = []
        for hh in range(2):
            in_head = (lane >= hh * FOX_DH) & (lane < (hh + 1) * FOX_DH)
            qm = jnp.where(in_head, q, jnp.zeros_like(q))
            f_t = jnp.sum(jnp.where(lane == 2 * hp + hh, f_blk, 0.0), axis=-1, keepdims=True)
            s = lax.dot_general(qm, k_ref[0, :klen, :], _NT, preferred_element_type=F32)
            s = s - ft_ref[0, 0, hh:hh + 1, :klen]
            s_diag = jnp.where(causal, s[:, q0:], -jnp.inf)
            m = jnp.max(s_diag, axis=-1, keepdims=True)
            if qi > 0:
                m = jnp.maximum(m, jnp.max(s[:, :q0], axis=-1, keepdims=True))
            shift = (m + f_t) - f_t
            p = jnp.exp(s_diag - shift)
            if qi > 0:
                p = jnp.concatenate([jnp.exp(s[:, :q0] - shift), p], axis=1)
            l = jnp.sum(p, axis=-1, keepdims=True)
            o = jnp.dot(p.astype(BF16), v_ref[0, :klen, :], preferred_element_type=F32)
            outs.append(o * (1.0 / l))
        o_ref[0, q0:klen, :] = jnp.where(lane < FOX_DH, outs[0], outs[1]).astype(o_ref.dtype)


def _fox_attn(fq, fk, fv, f_col, f_row):
    b, s, _ = fq.shape
    pairs = FOX_HEADS // 2
    f_row = f_row.reshape(b, pairs, 2, s)
    qkv = pl.BlockSpec((1, s, LANES), lambda i, h: (i, 0, h))
    return pl.pallas_call(
        _attn_kernel,
        out_shape=jax.ShapeDtypeStruct((b, s, FOX_WIDTH), BF16),
        grid=(b, pairs),
        in_specs=[qkv, qkv, qkv,
                  pl.BlockSpec((1, s, SMALL_W), lambda i, h: (i, 0, 0)),
                  pl.BlockSpec((1, 1, 2, s), lambda i, h: (i, h, 0, 0))],
        out_specs=qkv,
        compiler_params=pltpu.CompilerParams(
            dimension_semantics=("arbitrary", "arbitrary"), vmem_limit_bytes=VMEM_LIMIT),
        name="fox_attn",
    )(fq, fk, fv, f_col, f_row)


_TN = (((0,), (0,)), ((), ()))
_N_CHUNK = TG // CHUNK
_CHUNK_SHIFT = CHUNK.bit_length() - 1
_DK_SHIFT = GLA_DK.bit_length() - 1


def _gla_kernel(gq_ref, gk_ref, gv_ref, go_ref, sm_ref, w2_ref, b2_ref, gn_ref, o_ref, st_ref):
    @pl.when(pl.program_id(1) == 0)
    def _():
        st_ref[...] = jnp.zeros_like(st_ref)

    z = jnp.dot(sm_ref[0].astype(BF16), w2_ref[...], preferred_element_type=F32) + b2_ref[...]
    log_a = _log_sigmoid(z) / GLA_TEMP

    r = lax.broadcasted_iota(jnp.int32, (TG, TG), 0)
    c = lax.broadcasted_iota(jnp.int32, (TG, TG), 1)
    same_chunk = (r >> _CHUNK_SHIFT) == (c >> _CHUNK_SHIFT)
    tri = (same_chunk & (r >= c)).astype(BF16)
    ones = same_chunk.astype(BF16)
    hi, mid, lo = _split3(log_a)

    def sel_dot(sel):
        return (jnp.dot(sel, hi, preferred_element_type=F32)
                + jnp.dot(sel, mid, preferred_element_type=F32)
                + jnp.dot(sel, lo, preferred_element_type=F32))

    g_cum = sel_dot(tri)
    g_tot = sel_dot(ones)
    k_dec = (gk_ref[0] * jnp.exp(g_tot - g_cum)).astype(BF16)
    chunk_decay = jnp.exp(g_tot)

    q = gq_ref[0]
    lane_k = lax.broadcasted_iota(jnp.int32, (CHUNK, GLA_KW), 1) >> _DK_SHIFT
    lane_s = lax.broadcasted_iota(jnp.int32, (GLA_DV, GLA_KW), 1) >> _DK_SHIFT
    g_norm = gn_ref[...]

    state = st_ref[...]
    for n in range(_N_CHUNK):
        rows = slice(n * CHUNK, (n + 1) * CHUNK)
        full = lax.dot_general(gv_ref[0, rows, :], k_dec[rows], _TN, preferred_element_type=F32)
        delta = jnp.zeros((GLA_DV, GLA_KW), F32)
        for h in range(GLA_HEADS):
            delta = delta + jnp.where(lane_s == h, full[h * GLA_DV:(h + 1) * GLA_DV, :], 0.0)
        state = chunk_decay[n * CHUNK:n * CHUNK + 1, :] * state + delta
        qc = q[rows]
        q_stack = jnp.concatenate(
            [jnp.where(lane_k == h, qc, jnp.zeros_like(qc)) for h in range(GLA_HEADS)], axis=0)
        o_all = lax.dot_general(q_stack, state.astype(BF16), _NT, preferred_element_type=F32)
        for h in range(GLA_HEADS):
            o = o_all[h * CHUNK:(h + 1) * CHUNK, :]
            o = _rms(o, g_norm)
            gate = go_ref[0, rows, h * GLA_DV:(h + 1) * GLA_DV]
            o_ref[0, rows, h * GLA_DV:(h + 1) * GLA_DV] = (
                o * (gate * jax.nn.sigmoid(gate))).astype(o_ref.dtype)
    st_ref[...] = state


def _gla(gq, gk, gv, go, small, w2, b2, gn):
    b, s, _ = gq.shape

    def rows(width):
        return pl.BlockSpec((1, TG, width), lambda i, t: (i, t, 0))

    def const(shape):
        return pl.BlockSpec(shape, lambda i, t: (0, 0))

    return pl.pallas_call(
        _gla_kernel,
        out_shape=jax.ShapeDtypeStruct((b, s, GLA_VW), BF16),
        grid=(b, s // TG),
        in_specs=[rows(GLA_KW), rows(GLA_KW), rows(GLA_VW), rows(GLA_VW), rows(SMALL_W),
                  const((SMALL_W, GLA_KW)), const((1, GLA_KW)), const((1, GLA_DV))],
        out_specs=rows(GLA_VW),
        scratch_shapes=[pltpu.VMEM((GLA_DV, GLA_KW), F32)],
        compiler_params=pltpu.CompilerParams(dimension_semantics=("arbitrary", "arbitrary")),
        name="gla",
    )(gq, gk, gv, go, small, w2, b2, gn)


def _out_kernel(x_ref, fox_ref, gla_ref, wo_ref, g_ref, wg_ref, wu_ref, wd_ref, gf_ref, o_ref):
    x = x_ref[...]
    x = x + jnp.dot(fox_ref[...], wo_ref[:FOX_WIDTH, :], preferred_element_type=F32)
    x = x + jnp.dot(gla_ref[...], wo_ref[FOX_WIDTH:, :], preferred_element_type=F32)
    h = _rms(x, g_ref[...]).astype(BF16)
    x = x + 0.5 * _swiglu(h, wg_ref, wu_ref, wd_ref)
    o_ref[...] = _rms(x, gf_ref[...])


def _out_ffn2(x1, fox, gla, wo, g, wg, wu, wd, gf):
    n = x1.shape[0]

    def rows(width):
        return pl.BlockSpec((TM_FFN, width), lambda i: (i, 0))

    return pl.pallas_call(
        _out_kernel,
        out_shape=jax.ShapeDtypeStruct((n, D_MODEL), F32),
        grid=(n // TM_FFN,),
        in_specs=[rows(D_MODEL), rows(FOX_WIDTH), rows(GLA_VW),
                  _const_spec((FOX_WIDTH + GLA_VW, D_MODEL)), _const_spec((1, D_MODEL)),
                  _const_spec((D_MODEL, D_FF)), _const_spec((D_MODEL, D_FF)),
                  _const_spec((D_FF, D_MODEL)), _const_spec((1, D_MODEL))],
        out_specs=rows(D_MODEL),
        compiler_params=pltpu.CompilerParams(
            dimension_semantics=("arbitrary",), vmem_limit_bytes=VMEM_LIMIT),
        name="out_ffn2",
    )(x1, fox, gla, wo, g, wg, wu, wd, gf)


def _pack_w_in(w_in):
    widths = (FOX_WIDTH, FOX_WIDTH, FOX_WIDTH, FOX_HEADS, GLA_KW, GLA_KW, GLA_VW, GLA_RANK, GLA_VW)
    offs = [0]
    for wd in widths:
        offs.append(offs[-1] + wd)
    fq, fk, fv, ff, gq, gk, gv, glow, gout = (w_in[:, offs[i]:offs[i + 1]] for i in range(9))
    pad = jnp.zeros((D_MODEL, SMALL_W - FOX_HEADS - GLA_RANK), w_in.dtype)
    packed = jnp.concatenate(
        [fq * (FOX_DH ** -0.5), fk, fv, gq * (GLA_DK ** -0.5), gk, gv, gout, ff, glow, pad], axis=1)
    return packed.astype(BF16)


def kernel(x, ffn1_norm, ffn1_w_gate, ffn1_w_up, ffn1_w_down, mix_norm, w_in, fox_forget_bias,
           gla_w_gate_up, gla_gate_bias, gla_out_norm, w_out, ffn2_norm, ffn2_w_gate, ffn2_w_up,
           ffn2_w_down, final_norm):
    b, s, d = x.shape
    n = b * s
    assert ffn1_norm.shape[0] == 1, "the final norm is fused into the single layer's last call"
    x = x.reshape(n, d)
    for l in range(1):
        x1 = _ffn1(x, ffn1_norm[l].reshape(1, d), ffn1_w_gate[l].astype(BF16),
                   ffn1_w_up[l].astype(BF16), ffn1_w_down[l].astype(BF16))

        fq, fk, fv, gq, gk, gv, go, small = _in_proj(
            x1, mix_norm[l].reshape(1, d), _pack_w_in(w_in[l]))

        bias = jnp.zeros((1, SMALL_W), F32).at[0, :FOX_HEADS].set(fox_forget_bias[l])
        f_col, f_row = _fox_decay(small.reshape(b, s, SMALL_W), bias)
        fox = _fox_attn(fq.reshape(b, s, -1), fk.reshape(b, s, -1), fv.reshape(b, s, -1),
                        f_col, f_row)

        w2 = jnp.zeros((SMALL_W, GLA_KW), F32).at[FOX_HEADS:FOX_HEADS + GLA_RANK].set(
            gla_w_gate_up[l]).astype(BF16)
        gla = _gla(gq.reshape(b, s, -1), gk.reshape(b, s, -1), gv.reshape(b, s, -1),
                   go.reshape(b, s, -1), small.reshape(b, s, SMALL_W), w2,
                   gla_gate_bias[l].reshape(1, GLA_KW), gla_out_norm[l].reshape(1, GLA_DV))

        x = _out_ffn2(x1, fox.reshape(n, -1), gla.reshape(n, -1), w_out[l].astype(BF16),
                      ffn2_norm[l].reshape(1, d), ffn2_w_gate[l].astype(BF16),
                      ffn2_w_up[l].astype(BF16), ffn2_w_down[l].astype(BF16),
                      final_norm.reshape(1, d))
    return x.reshape(b, s, d)
```

```python
import functools

import jax
import jax.numpy as jnp
from jax import lax
from jax.experimental import pallas as pl
from jax.experimental.pallas import tpu as pltpu

F32 = jnp.float32
BF16 = jnp.bfloat16

D_MODEL = 1024
D_FF = 2816
EPS = 1e-6
CHUNK = 64

FOX_HEADS = 8
FOX_DH = 64
FOX_WIDTH = FOX_HEADS * FOX_DH
GLA_HEADS = 4
GLA_DK = 64
GLA_DV = 128
GLA_KW = GLA_HEADS * GLA_DK
GLA_VW = GLA_HEADS * GLA_DV
GLA_RANK = 16
GLA_TEMP = 16.0

LANES = 128
SMALL_W = LANES
_C_FQ = 0
_C_FK = _C_FQ + FOX_WIDTH
_C_FV = _C_FK + FOX_WIDTH
_C_GQ = _C_FV + FOX_WIDTH
_C_GK = _C_GQ + GLA_KW
_C_GV = _C_GK + GLA_KW
_C_GO = _C_GV + GLA_VW
_C_SM = _C_GO + GLA_VW
PROJ_W = _C_SM + SMALL_W

TM_FFN = 512
TM_PROJ = 512
TQ = 256
TG = 512
VMEM_LIMIT = 56 * 1024 * 1024


def _rms(x, g):
    return x * lax.rsqrt(jnp.mean(x * x, axis=-1, keepdims=True) + EPS) * g


def _log_sigmoid(z):
    return jnp.minimum(z, 0.0) - jnp.log1p(jnp.exp(-jnp.abs(z)))


def _swiglu(h, wg_ref, wu_ref, wd_ref):
    gate = jnp.dot(h, wg_ref[...], preferred_element_type=F32)
    up = jnp.dot(h, wu_ref[...], preferred_element_type=F32)
    act = (gate * jax.nn.sigmoid(gate) * up).astype(BF16)
    return jnp.dot(act, wd_ref[...], preferred_element_type=F32)


def _split3(x):
    hi = x.astype(BF16)
    r1 = x - hi.astype(F32)
    mid = r1.astype(BF16)
    lo = (r1 - mid.astype(F32)).astype(BF16)
    return hi, mid, lo


def _dot3(sel, x):
    hi, mid, lo = _split3(x)
    return (jnp.dot(sel, hi, preferred_element_type=F32)
            + jnp.dot(sel, mid, preferred_element_type=F32)
            + jnp.dot(sel, lo, preferred_element_type=F32))


def _const_spec(shape):
    n = len(shape)
    return pl.BlockSpec(shape, lambda *_: (0,) * n, pipeline_mode=pl.Buffered(1))


def _ffn1_kernel(x_ref, g_ref, wg_ref, wu_ref, wd_ref, o_ref):
    x = x_ref[...]
    h = _rms(x, g_ref[...]).astype(BF16)
    o_ref[...] = x + 0.5 * _swiglu(h, wg_ref, wu_ref, wd_ref)


def _ffn1(x, g, wg, wu, wd):
    n = x.shape[0]
    row = pl.BlockSpec((TM_FFN, D_MODEL), lambda i: (i, 0))
    return pl.pallas_call(
        _ffn1_kernel,
        out_shape=jax.ShapeDtypeStruct((n, D_MODEL), F32),
        grid=(n // TM_FFN,),
        in_specs=[row, _const_spec((1, D_MODEL)), _const_spec((D_MODEL, D_FF)),
                  _const_spec((D_MODEL, D_FF)), _const_spec((D_FF, D_MODEL))],
        out_specs=row,
        compiler_params=pltpu.CompilerParams(
            dimension_semantics=("arbitrary",), vmem_limit_bytes=VMEM_LIMIT),
        name="ffn1",
    )(x, g, wg, wu, wd)


def _proj_kernel(x_ref, g_ref, w_ref, fq_ref, fk_ref, fv_ref, gq_ref, gk_ref, gv_ref,
                 go_ref, sm_ref):
    h = _rms(x_ref[...], g_ref[...]).astype(BF16)

    def proj(c0, width):
        return jnp.dot(h, w_ref[:, c0:c0 + width], preferred_element_type=F32)

    fq_ref[...] = proj(_C_FQ, FOX_WIDTH).astype(BF16)
    fk_ref[...] = proj(_C_FK, FOX_WIDTH).astype(BF16)
    fv_ref[...] = proj(_C_FV, FOX_WIDTH).astype(BF16)
    gq_ref[...] = proj(_C_GQ, GLA_KW).astype(BF16)
    gk_ref[...] = proj(_C_GK, GLA_KW)
    gv_ref[...] = proj(_C_GV, GLA_VW).astype(BF16)
    go_ref[...] = proj(_C_GO, GLA_VW)
    sm_ref[...] = proj(_C_SM, SMALL_W)


def _in_proj(x1, g, w):
    n = x1.shape[0]

    def rows(width):
        return pl.BlockSpec((TM_PROJ, width), lambda i: (i, 0))

    widths = (FOX_WIDTH, FOX_WIDTH, FOX_WIDTH, GLA_KW, GLA_KW, GLA_VW, GLA_VW, SMALL_W)
    dtypes = (BF16, BF16, BF16, BF16, F32, BF16, F32, F32)
    return pl.pallas_call(
        _proj_kernel,
        out_shape=[jax.ShapeDtypeStruct((n, wd), dt) for wd, dt in zip(widths, dtypes)],
        grid=(n // TM_PROJ,),
        in_specs=[rows(D_MODEL), _const_spec((1, D_MODEL)), _const_spec((D_MODEL, PROJ_W))],
        out_specs=[rows(wd) for wd in widths],
        compiler_params=pltpu.CompilerParams(
            dimension_semantics=("arbitrary",), vmem_limit_bytes=VMEM_LIMIT),
        name="in_proj",
    )(x1, g, w)


_DECAY_BLK = 256
_N_TERMS = 3
_BIAS_LANES = _N_TERMS * FOX_HEADS
_FF_LANES = 2 * _BIAS_LANES
_LOG2E = 1.4426950408889634


def _decay_kernel(sm_ref, b_ref, ka_ref, qa_ref):
    seq = sm_ref.shape[1]
    r = lax.broadcasted_iota(jnp.int32, (_DECAY_BLK, _DECAY_BLK), 0)
    c = lax.broadcasted_iota(jnp.int32, (_DECAY_BLK, _DECAY_BLK), 1)
    tri = (r >= c).astype(BF16)
    lane = lax.broadcasted_iota(jnp.int32, (_DECAY_BLK, SMALL_W), 1)
    term = (lane >> 3) % _N_TERMS
    carry = jnp.zeros((1, SMALL_W), F32)
    for blk in range(seq // _DECAY_BLK):
        rows = slice(blk * _DECAY_BLK, (blk + 1) * _DECAY_BLK)
        log_f = _log_sigmoid(sm_ref[0, rows, :] + b_ref[...])
        log_f = jnp.where(lane < _FF_LANES, log_f, 0.0)
        cum = _dot3(tri, log_f) + carry
        carry = cum[_DECAY_BLK - 1:_DECAY_BLK, :]
        hi, mid, lo = _split3(cum * _LOG2E)
        f_term = jnp.where(term == 0, hi.astype(F32),
                           jnp.where(term == 1, mid.astype(F32), lo.astype(F32)))
        one = jnp.ones_like(f_term)
        zero = jnp.zeros_like(f_term)
        ka_ref[0, rows, :] = jnp.where(
            lane < _BIAS_LANES, -f_term, jnp.where(lane < _FF_LANES, one, zero)).astype(BF16)
        qa_ref[0, rows, :] = jnp.where(
            lane < _BIAS_LANES, one, jnp.where(lane < _FF_LANES, f_term, zero)).astype(BF16)


def _fox_decay(small, bias):
    b, s, _ = small.shape
    blk = pl.BlockSpec((1, s, SMALL_W), lambda i: (i, 0, 0))
    return pl.pallas_call(
        _decay_kernel,
        out_shape=[jax.ShapeDtypeStruct((b, s, SMALL_W), BF16)] * 2,
        grid=(b,),
        in_specs=[blk, pl.BlockSpec((1, SMALL_W), lambda i: (0, 0))],
        out_specs=[blk, blk],
        compiler_params=pltpu.CompilerParams(dimension_semantics=("arbitrary",)),
        name="fox_decay",
    )(small, bias)


_NT = (((1,), (1,)), ((), ()))
_LOOKAHEAD = 2


def _attn_kernel(q_ref, k_ref, v_ref, qa_ref, ka_ref, o_ref):
    hp = pl.program_id(1)
    seq = q_ref.shape[1]
    lane = lax.broadcasted_iota(jnp.int32, (TQ, LANES), 1)
    k_idx = lax.broadcasted_iota(jnp.int32, (TQ, TQ), 0)
    q_idx = lax.broadcasted_iota(jnp.int32, (TQ, TQ), 1)
    causal = k_idx <= q_idx

    k_aug = jnp.concatenate([k_ref[0], ka_ref[0]], axis=1)
    v_t = v_ref[0].astype(F32).T.astype(BF16)

    def logits(qi):
        q0 = qi * TQ
        klen = q0 + TQ
        q = q_ref[0, q0:klen, :]
        qa = qa_ref[0, q0:klen, :]
        out = []
        for hh in range(2):
            in_head = (lane >= hh * FOX_DH) & (lane < (hh + 1) * FOX_DH)
            bias_lane = ((lane & (FOX_HEADS - 1)) == 2 * hp + hh) & (lane < _FF_LANES)
            q_aug = jnp.concatenate([jnp.where(in_head, q, jnp.zeros_like(q)),
                                     jnp.where(bias_lane, qa, jnp.zeros_like(qa))], axis=1)
            out.append(lax.dot_general(k_aug[:klen], q_aug, _NT, preferred_element_type=F32))
        return out

    def finish(qi, s_both):
        q0 = qi * TQ
        klen = q0 + TQ
        outs = []
        for hh in range(2):
            s_t = s_both[hh]
            s_diag = jnp.where(causal, s_t[q0:], -jnp.inf)
            m = jnp.max(s_diag, axis=0, keepdims=True)
            if qi > 0:
                m = jnp.maximum(m, jnp.max(s_t[:q0], axis=0, keepdims=True))
            p_t = jnp.exp2(s_diag - m)
            if qi > 0:
                p_t = jnp.concatenate([jnp.exp2(s_t[:q0] - m), p_t], axis=0)
            l = jnp.sum(p_t, axis=0, keepdims=True)
            o_t = jnp.dot(v_t[hh * FOX_DH:(hh + 1) * FOX_DH, :klen], p_t.astype(BF16),
                          preferred_element_type=F32)
            outs.append(o_t * (1.0 / l))
        o_ref[0, q0:klen, :] = jnp.concatenate(outs, axis=0).T.astype(o_ref.dtype)

    n_blocks = seq // TQ
    order = list(range(1, n_blocks, 2)) + list(range(n_blocks - 2 + n_blocks % 2, -1, -2))
    ready = {qi: logits(qi) for qi in order[:_LOOKAHEAD]}
    for pos, qi in enumerate(order):
        if pos + _LOOKAHEAD < len(order):
            ahead = order[pos + _LOOKAHEAD]
            ready[ahead] = logits(ahead)
        finish(qi, ready.pop(qi))


def _fox_attn(fq, fk, fv, qa, ka):
    b, s, _ = fq.shape
    qkv = pl.BlockSpec((1, s, LANES), lambda i, h: (i, 0, h))
    aug = pl.BlockSpec((1, s, SMALL_W), lambda i, h: (i, 0, 0))
    return pl.pallas_call(
        _attn_kernel,
        out_shape=jax.ShapeDtypeStruct((b, s, FOX_WIDTH), BF16),
        grid=(b, FOX_HEADS // 2),
        in_specs=[qkv, qkv, qkv, aug, aug],
        out_specs=qkv,
        compiler_params=pltpu.CompilerParams(
            dimension_semantics=("arbitrary", "arbitrary"), vmem_limit_bytes=VMEM_LIMIT),
        name="fox_attn",
    )(fq, fk, fv, qa, ka)


_TN = (((0,), (0,)), ((), ()))
_N_CHUNK = TG // CHUNK
_CHUNK_SHIFT = CHUNK.bit_length() - 1
_DK_SHIFT = GLA_DK.bit_length() - 1


def _gla_kernel(gq_ref, gk_ref, gv_ref, go_ref, sm_ref, w2_ref, b2_ref, gn_ref, o_ref, st_ref):
    @pl.when(pl.program_id(1) == 0)
    def _():
        st_ref[...] = jnp.zeros_like(st_ref)

    z = jnp.dot(sm_ref[0].astype(BF16), w2_ref[...], preferred_element_type=F32) + b2_ref[...]
    log_a = _log_sigmoid(z) / GLA_TEMP

    r = lax.broadcasted_iota(jnp.int32, (TG, TG), 0)
    c = lax.broadcasted_iota(jnp.int32, (TG, TG), 1)
    same_chunk = (r >> _CHUNK_SHIFT) == (c >> _CHUNK_SHIFT)
    tri = (same_chunk & (r >= c)).astype(BF16)
    ones = same_chunk.astype(BF16)
    hi, mid, lo = _split3(log_a)

    def sel_dot(sel):
        return (jnp.dot(sel, hi, preferred_element_type=F32)
                + jnp.dot(sel, mid, preferred_element_type=F32)
                + jnp.dot(sel, lo, preferred_element_type=F32))

    g_cum = sel_dot(tri)
    g_tot = sel_dot(ones)
    k_dec = (gk_ref[0] * jnp.exp(g_tot - g_cum)).astype(BF16)
    chunk_decay = jnp.exp(g_tot)

    q = gq_ref[0]
    lane_k = lax.broadcasted_iota(jnp.int32, (CHUNK, GLA_KW), 1) >> _DK_SHIFT
    lane_s = lax.broadcasted_iota(jnp.int32, (GLA_DV, GLA_KW), 1) >> _DK_SHIFT
    g_norm = gn_ref[...]

    state = st_ref[...]
    for n in range(_N_CHUNK):
        rows = slice(n * CHUNK, (n + 1) * CHUNK)
        full = lax.dot_general(gv_ref[0, rows, :], k_dec[rows], _TN, preferred_element_type=F32)
        delta = jnp.zeros((GLA_DV, GLA_KW), F32)
        for h in range(GLA_HEADS):
            delta = delta + jnp.where(lane_s == h, full[h * GLA_DV:(h + 1) * GLA_DV, :], 0.0)
        state = chunk_decay[n * CHUNK:n * CHUNK + 1, :] * state + delta
        qc = q[rows]
        q_stack = jnp.concatenate(
            [jnp.where(lane_k == h, qc, jnp.zeros_like(qc)) for h in range(GLA_HEADS)], axis=0)
        o_all = lax.dot_general(q_stack, state.astype(BF16), _NT, preferred_element_type=F32)
        for h in range(GLA_HEADS):
            o = o_all[h * CHUNK:(h + 1) * CHUNK, :]
            o = _rms(o, g_norm)
            gate = go_ref[0, rows, h * GLA_DV:(h + 1) * GLA_DV]
            o_ref[0, rows, h * GLA_DV:(h + 1) * GLA_DV] = (
                o * (gate * jax.nn.sigmoid(gate))).astype(o_ref.dtype)
    st_ref[...] = state


def _gla(gq, gk, gv, go, small, w2, b2, gn):
    b, s, _ = gq.shape

    def rows(width):
        return pl.BlockSpec((1, TG, width), lambda i, t: (i, t, 0))

    def const(shape):
        return pl.BlockSpec(shape, lambda i, t: (0, 0))

    return pl.pallas_call(
        _gla_kernel,
        out_shape=jax.ShapeDtypeStruct((b, s, GLA_VW), BF16),
        grid=(b, s // TG),
        in_specs=[rows(GLA_KW), rows(GLA_KW), rows(GLA_VW), rows(GLA_VW), rows(SMALL_W),
                  const((SMALL_W, GLA_KW)), const((1, GLA_KW)), const((1, GLA_DV))],
        out_specs=rows(GLA_VW),
        scratch_shapes=[pltpu.VMEM((GLA_DV, GLA_KW), F32)],
        compiler_params=pltpu.CompilerParams(dimension_semantics=("arbitrary", "arbitrary")),
        name="gla",
    )(gq, gk, gv, go, small, w2, b2, gn)


def _out_kernel(x_ref, fox_ref, gla_ref, wo_ref, g_ref, wg_ref, wu_ref, wd_ref, gf_ref, o_ref):
    x = x_ref[...]
    x = x + jnp.dot(fox_ref[...], wo_ref[:FOX_WIDTH, :], preferred_element_type=F32)
    x = x + jnp.dot(gla_ref[...], wo_ref[FOX_WIDTH:, :], preferred_element_type=F32)
    h = _rms(x, g_ref[...]).astype(BF16)
    x = x + 0.5 * _swiglu(h, wg_ref, wu_ref, wd_ref)
    o_ref[...] = _rms(x, gf_ref[...])


def _out_ffn2(x1, fox, gla, wo, g, wg, wu, wd, gf):
    n = x1.shape[0]

    def rows(width):
        return pl.BlockSpec((TM_FFN, width), lambda i: (i, 0))

    return pl.pallas_call(
        _out_kernel,
        out_shape=jax.ShapeDtypeStruct((n, D_MODEL), F32),
        grid=(n // TM_FFN,),
        in_specs=[rows(D_MODEL), rows(FOX_WIDTH), rows(GLA_VW),
                  _const_spec((FOX_WIDTH + GLA_VW, D_MODEL)), _const_spec((1, D_MODEL)),
                  _const_spec((D_MODEL, D_FF)), _const_spec((D_MODEL, D_FF)),
                  _const_spec((D_FF, D_MODEL)), _const_spec((1, D_MODEL))],
        out_specs=rows(D_MODEL),
        compiler_params=pltpu.CompilerParams(
            dimension_semantics=("arbitrary",), vmem_limit_bytes=VMEM_LIMIT),
        name="out_ffn2",
    )(x1, fox, gla, wo, g, wg, wu, wd, gf)


def _pack_w_in(w_in):
    widths = (FOX_WIDTH, FOX_WIDTH, FOX_WIDTH, FOX_HEADS, GLA_KW, GLA_KW, GLA_VW, GLA_RANK, GLA_VW)
    offs = [0]
    for wd in widths:
        offs.append(offs[-1] + wd)
    fq, fk, fv, ff, gq, gk, gv, glow, gout = (w_in[:, offs[i]:offs[i + 1]] for i in range(9))
    pad = jnp.zeros((D_MODEL, SMALL_W - _FF_LANES - GLA_RANK), w_in.dtype)
    packed = jnp.concatenate(
        [fq * (_LOG2E * FOX_DH ** -0.5), fk, fv, gq * (GLA_DK ** -0.5), gk, gv, gout,
         jnp.tile(ff, (1, _FF_LANES // FOX_HEADS)), glow, pad], axis=1)
    return packed.astype(BF16)


def kernel(x, ffn1_norm, ffn1_w_gate, ffn1_w_up, ffn1_w_down, mix_norm, w_in, fox_forget_bias,
           gla_w_gate_up, gla_gate_bias, gla_out_norm, w_out, ffn2_norm, ffn2_w_gate, ffn2_w_up,
           ffn2_w_down, final_norm):
    b, s, d = x.shape
    n = b * s
    assert ffn1_norm.shape[0] == 1, "the final norm is fused into the single layer's last call"
    x = x.reshape(n, d)
    for l in range(1):
        x1 = _ffn1(x, ffn1_norm[l].reshape(1, d), ffn1_w_gate[l].astype(BF16),
                   ffn1_w_up[l].astype(BF16), ffn1_w_down[l].astype(BF16))

        fq, fk, fv, gq, gk, gv, go, small = _in_proj(
            x1, mix_norm[l].reshape(1, d), _pack_w_in(w_in[l]))

        bias = jnp.zeros((1, SMALL_W), F32).at[0, :_FF_LANES].set(
            jnp.tile(fox_forget_bias[l], _FF_LANES // FOX_HEADS))
        ka, qa = _fox_decay(small.reshape(b, s, SMALL_W), bias)
        fox = _fox_attn(fq.reshape(b, s, -1), fk.reshape(b, s, -1), fv.reshape(b, s, -1), qa, ka)

        w2 = jnp.zeros((SMALL_W, GLA_KW), F32).at[_FF_LANES:_FF_LANES + GLA_RANK].set(
            gla_w_gate_up[l]).astype(BF16)
        gla = _gla(gq.reshape(b, s, -1), gk.reshape(b, s, -1), gv.reshape(b, s, -1),
                   go.reshape(b, s, -1), small.reshape(b, s, SMALL_W), w2,
                   gla_gate_bias[l].reshape(1, GLA_KW), gla_out_norm[l].reshape(1, GLA_DV))

        x = _out_ffn2(x1, fox.reshape(n, -1), gla.reshape(n, -1), w_out[l].astype(BF16),
                      ffn2_norm[l].reshape(1, d), ffn2_w_gate[l].astype(BF16),
                      ffn2_w_up[l].astype(BF16), ffn2_w_down[l].astype(BF16),
                      final_norm.reshape(1, d))
    return x.reshape(b, s, d)
```

```python
import functools

import jax
import jax.numpy as jnp
from jax import lax
from jax.experimental import pallas as pl
from jax.experimental.pallas import tpu as pltpu

F32 = jnp.float32
BF16 = jnp.bfloat16

D_MODEL = 1024
D_FF = 2816
EPS = 1e-6
CHUNK = 64

FOX_HEADS = 8
FOX_DH = 64
FOX_WIDTH = FOX_HEADS * FOX_DH
GLA_HEADS = 4
GLA_DK = 64
GLA_DV = 128
GLA_KW = GLA_HEADS * GLA_DK
GLA_VW = GLA_HEADS * GLA_DV
GLA_RANK = 16
GLA_TEMP = 16.0

LANES = 128
SMALL_W = LANES
_C_FQ = 0
_C_FK = _C_FQ + FOX_WIDTH
_C_FV = _C_FK + FOX_WIDTH
_C_GQ = _C_FV + FOX_WIDTH
_C_GK = _C_GQ + GLA_KW
_C_GV = _C_GK + GLA_KW
_C_GO = _C_GV + GLA_VW
_C_SM = _C_GO + GLA_VW
PROJ_W = _C_SM + SMALL_W

TM_FFN = 512
TM_PROJ = 512
TQ = 256
TG = 1024
VMEM_LIMIT = 56 * 1024 * 1024


def _rms(x, g):
    return x * lax.rsqrt(jnp.mean(x * x, axis=-1, keepdims=True) + EPS) * g


def _log_sigmoid(z):
    return jnp.minimum(z, 0.0) - jnp.log(1.0 + jnp.exp(-jnp.abs(z)))


def _swiglu(h, wg_ref, wu_ref, wd_ref):
    gate = jnp.dot(h, wg_ref[...], preferred_element_type=F32)
    up = jnp.dot(h, wu_ref[...], preferred_element_type=F32)
    act = (gate * jax.nn.sigmoid(gate) * up).astype(BF16)
    return jnp.dot(act, wd_ref[...], preferred_element_type=F32)


def _split3(x):
    hi = x.astype(BF16)
    r1 = x - hi.astype(F32)
    mid = r1.astype(BF16)
    lo = (r1 - mid.astype(F32)).astype(BF16)
    return hi, mid, lo


def _dot3(sel, x):
    hi, mid, lo = _split3(x)
    return (jnp.dot(sel, hi, preferred_element_type=F32)
            + jnp.dot(sel, mid, preferred_element_type=F32)
            + jnp.dot(sel, lo, preferred_element_type=F32))


def _const_spec(shape):
    n = len(shape)
    return pl.BlockSpec(shape, lambda *_: (0,) * n, pipeline_mode=pl.Buffered(1))


def _ffn1_kernel(x_ref, g_ref, wg_ref, wu_ref, wd_ref, o_ref):
    x = x_ref[...]
    h = _rms(x, g_ref[...]).astype(BF16)
    o_ref[...] = x + 0.5 * _swiglu(h, wg_ref, wu_ref, wd_ref)


def _ffn1(x, g, wg, wu, wd):
    n = x.shape[0]
    row = pl.BlockSpec((TM_FFN, D_MODEL), lambda i: (i, 0))
    return pl.pallas_call(
        _ffn1_kernel,
        out_shape=jax.ShapeDtypeStruct((n, D_MODEL), F32),
        grid=(n // TM_FFN,),
        in_specs=[row, _const_spec((1, D_MODEL)), _const_spec((D_MODEL, D_FF)),
                  _const_spec((D_MODEL, D_FF)), _const_spec((D_FF, D_MODEL))],
        out_specs=row,
        compiler_params=pltpu.CompilerParams(
            dimension_semantics=("arbitrary",), vmem_limit_bytes=VMEM_LIMIT),
        name="ffn1",
    )(x, g, wg, wu, wd)


def _proj_kernel(x_ref, g_ref, w_ref, fq_ref, fk_ref, fv_ref, gq_ref, gk_ref, gv_ref,
                 go_ref, sm_ref):
    h = _rms(x_ref[...], g_ref[...]).astype(BF16)

    def proj(c0, width):
        return jnp.dot(h, w_ref[:, c0:c0 + width], preferred_element_type=F32)

    fq_ref[...] = proj(_C_FQ, FOX_WIDTH).astype(BF16)
    fk_ref[...] = proj(_C_FK, FOX_WIDTH).astype(BF16)
    fv_ref[...] = proj(_C_FV, FOX_WIDTH).astype(BF16)
    gq_ref[...] = proj(_C_GQ, GLA_KW).astype(BF16)
    gk_ref[...] = proj(_C_GK, GLA_KW)
    gv_ref[...] = proj(_C_GV, GLA_VW).astype(BF16)
    go_ref[...] = proj(_C_GO, GLA_VW)
    sm_ref[...] = proj(_C_SM, SMALL_W)


def _in_proj(x1, g, w):
    n = x1.shape[0]

    def rows(width):
        return pl.BlockSpec((TM_PROJ, width), lambda i: (i, 0))

    widths = (FOX_WIDTH, FOX_WIDTH, FOX_WIDTH, GLA_KW, GLA_KW, GLA_VW, GLA_VW, SMALL_W)
    dtypes = (BF16, BF16, BF16, BF16, F32, BF16, F32, F32)
    return pl.pallas_call(
        _proj_kernel,
        out_shape=[jax.ShapeDtypeStruct((n, wd), dt) for wd, dt in zip(widths, dtypes)],
        grid=(n // TM_PROJ,),
        in_specs=[rows(D_MODEL), _const_spec((1, D_MODEL)), _const_spec((D_MODEL, PROJ_W))],
        out_specs=[rows(wd) for wd in widths],
        compiler_params=pltpu.CompilerParams(
            dimension_semantics=("arbitrary",), vmem_limit_bytes=VMEM_LIMIT),
        name="in_proj",
    )(x1, g, w)


_DECAY_BLK = 256
_N_TERMS = 3
_BIAS_LANES = _N_TERMS * FOX_HEADS
_FF_LANES = 2 * _BIAS_LANES
_LOG2E = 1.4426950408889634


def _decay_kernel(sm_ref, b_ref, ka_ref, qa_ref):
    seq = sm_ref.shape[1]
    r = lax.broadcasted_iota(jnp.int32, (_DECAY_BLK, _DECAY_BLK), 0)
    c = lax.broadcasted_iota(jnp.int32, (_DECAY_BLK, _DECAY_BLK), 1)
    tri = (r >= c).astype(BF16)
    lane = lax.broadcasted_iota(jnp.int32, (_DECAY_BLK, SMALL_W), 1)
    term = (lane >> 3) % _N_TERMS
    carry = jnp.zeros((1, SMALL_W), F32)
    for blk in range(seq // _DECAY_BLK):
        rows = slice(blk * _DECAY_BLK, (blk + 1) * _DECAY_BLK)
        log_f = _log_sigmoid(sm_ref[0, rows, :] + b_ref[...])
        log_f = jnp.where(lane < _FF_LANES, log_f, 0.0)
        cum = _dot3(tri, log_f) + carry
        carry = cum[_DECAY_BLK - 1:_DECAY_BLK, :]
        hi, mid, lo = _split3(cum * _LOG2E)
        f_term = jnp.where(term == 0, hi.astype(F32),
                           jnp.where(term == 1, mid.astype(F32), lo.astype(F32)))
        one = jnp.ones_like(f_term)
        zero = jnp.zeros_like(f_term)
        ka_ref[0, rows, :] = jnp.where(
            lane < _BIAS_LANES, -f_term, jnp.where(lane < _FF_LANES, one, zero)).astype(BF16)
        qa_ref[0, rows, :] = jnp.where(
            lane < _BIAS_LANES, one, jnp.where(lane < _FF_LANES, f_term, zero)).astype(BF16)


def _fox_decay(small, bias):
    b, s, _ = small.shape
    blk = pl.BlockSpec((1, s, SMALL_W), lambda i: (i, 0, 0))
    return pl.pallas_call(
        _decay_kernel,
        out_shape=[jax.ShapeDtypeStruct((b, s, SMALL_W), BF16)] * 2,
        grid=(b,),
        in_specs=[blk, pl.BlockSpec((1, SMALL_W), lambda i: (0, 0))],
        out_specs=[blk, blk],
        compiler_params=pltpu.CompilerParams(dimension_semantics=("arbitrary",)),
        name="fox_decay",
    )(small, bias)


_NT = (((1,), (1,)), ((), ()))
_LOOKAHEAD = 2


def _attn_kernel(q_ref, k_ref, v_ref, qa_ref, ka_ref, o_ref):
    hp = pl.program_id(1)
    seq = q_ref.shape[1]
    lane = lax.broadcasted_iota(jnp.int32, (TQ, LANES), 1)
    k_idx = lax.broadcasted_iota(jnp.int32, (TQ, TQ), 0)
    q_idx = lax.broadcasted_iota(jnp.int32, (TQ, TQ), 1)
    causal = k_idx <= q_idx

    k_aug = jnp.concatenate([k_ref[0], ka_ref[0]], axis=1)
    v_t = v_ref[0].astype(F32).T.astype(BF16)

    def logits(qi):
        q0 = qi * TQ
        klen = q0 + TQ
        q = q_ref[0, q0:klen, :]
        qa = qa_ref[0, q0:klen, :]
        out = []
        for hh in range(2):
            in_head = (lane >= hh * FOX_DH) & (lane < (hh + 1) * FOX_DH)
            bias_lane = ((lane & (FOX_HEADS - 1)) == 2 * hp + hh) & (lane < _FF_LANES)
            q_aug = jnp.concatenate([jnp.where(in_head, q, jnp.zeros_like(q)),
                                     jnp.where(bias_lane, qa, jnp.zeros_like(qa))], axis=1)
            out.append(lax.dot_general(k_aug[:klen], q_aug, _NT, preferred_element_type=F32))
        return out

    def finish(qi, s_both):
        q0 = qi * TQ
        klen = q0 + TQ
        outs = []
        for hh in range(2):
            s_t = s_both[hh]
            s_diag = jnp.where(causal, s_t[q0:], -jnp.inf)
            m = jnp.max(s_diag, axis=0, keepdims=True)
            if qi > 0:
                m = jnp.maximum(m, jnp.max(s_t[:q0], axis=0, keepdims=True))
            p_t = jnp.exp2(s_diag - m)
            if qi > 0:
                p_t = jnp.concatenate([jnp.exp2(s_t[:q0] - m), p_t], axis=0)
            l = jnp.sum(p_t, axis=0, keepdims=True)
            o_t = jnp.dot(v_t[hh * FOX_DH:(hh + 1) * FOX_DH, :klen], p_t.astype(BF16),
                          preferred_element_type=F32)
            outs.append(o_t * (1.0 / l))
        o_ref[0, q0:klen, :] = jnp.concatenate(outs, axis=0).T.astype(o_ref.dtype)

    n_blocks = seq // TQ
    order = list(range(1, n_blocks, 2)) + list(range(n_blocks - 2 + n_blocks % 2, -1, -2))
    ready = {qi: logits(qi) for qi in order[:_LOOKAHEAD]}
    for pos, qi in enumerate(order):
        if pos + _LOOKAHEAD < len(order):
            ahead = order[pos + _LOOKAHEAD]
            ready[ahead] = logits(ahead)
        finish(qi, ready.pop(qi))


def _fox_attn(fq, fk, fv, qa, ka):
    b, s, _ = fq.shape
    qkv = pl.BlockSpec((1, s, LANES), lambda i, h: (i, 0, h))
    aug = pl.BlockSpec((1, s, SMALL_W), lambda i, h: (i, 0, 0))
    return pl.pallas_call(
        _attn_kernel,
        out_shape=jax.ShapeDtypeStruct((b, s, FOX_WIDTH), BF16),
        grid=(b, FOX_HEADS // 2),
        in_specs=[qkv, qkv, qkv, aug, aug],
        out_specs=qkv,
        compiler_params=pltpu.CompilerParams(
            dimension_semantics=("arbitrary", "arbitrary"), vmem_limit_bytes=VMEM_LIMIT),
        name="fox_attn",
    )(fq, fk, fv, qa, ka)


_TN = (((0,), (0,)), ((), ()))
_N_CHUNK = TG // CHUNK
_CHUNK_SHIFT = CHUNK.bit_length() - 1
_DK_SHIFT = GLA_DK.bit_length() - 1


def _gla_kernel(gq_ref, gk_ref, gv_ref, go_ref, sm_ref, w2_ref, b2_ref, gn_ref, o_ref, st_ref):
    @pl.when(pl.program_id(1) == 0)
    def _():
        st_ref[...] = jnp.zeros_like(st_ref)

    z = jnp.dot(sm_ref[0].astype(BF16), w2_ref[...], preferred_element_type=F32) + b2_ref[...]
    log_a = _log_sigmoid(z) / GLA_TEMP

    r = lax.broadcasted_iota(jnp.int32, (CHUNK, CHUNK), 0)
    c = lax.broadcasted_iota(jnp.int32, (CHUNK, CHUNK), 1)
    tri = (r >= c).astype(BF16)
    terms = jnp.concatenate(_split3(log_a), axis=1)
    chunks = [slice(n * CHUNK, (n + 1) * CHUNK) for n in range(_N_CHUNK)]
    lane_k = lax.broadcasted_iota(jnp.int32, (CHUNK, GLA_KW), 1) >> _DK_SHIFT

    deltas, decays = [], []
    for rows in chunks:
        cum3 = jnp.dot(tri, terms[rows], preferred_element_type=F32)
        g_cum = cum3[:, :GLA_KW] + cum3[:, GLA_KW:2 * GLA_KW] + cum3[:, 2 * GLA_KW:]
        g_tot = g_cum[CHUNK - 1:CHUNK, :]
        k_dec = (gk_ref[0, rows, :] * jnp.exp(g_tot - g_cum)).astype(BF16)
        delta = None
        for h in range(GLA_HEADS):
            part = lax.dot_general(gv_ref[0, rows, h * GLA_DV:(h + 1) * GLA_DV],
                                   jnp.where(lane_k == h, k_dec, jnp.zeros_like(k_dec)), _TN,
                                   preferred_element_type=F32)
            delta = part if delta is None else delta + part
        deltas.append(delta)
        decays.append(jnp.exp(g_tot))

    state = st_ref[...]
    states = []
    for delta, decay in zip(deltas, decays):
        state = decay * state + delta
        states.append(state.astype(BF16))
    st_ref[...] = state

    g_norm = gn_ref[...]
    for rows, st in zip(chunks, states):
        qc = gq_ref[0, rows, :]
        q_stack = jnp.concatenate(
            [jnp.where(lane_k == h, qc, jnp.zeros_like(qc)) for h in range(GLA_HEADS)], axis=0)
        o_all = lax.dot_general(q_stack, st, _NT, preferred_element_type=F32)
        for h in range(GLA_HEADS):
            o = _rms(o_all[h * CHUNK:(h + 1) * CHUNK, :], g_norm)
            gate = go_ref[0, rows, h * GLA_DV:(h + 1) * GLA_DV]
            o_ref[0, rows, h * GLA_DV:(h + 1) * GLA_DV] = (
                o * (gate * jax.nn.sigmoid(gate))).astype(o_ref.dtype)


def _gla(gq, gk, gv, go, small, w2, b2, gn):
    b, s, _ = gq.shape

    def rows(width):
        return pl.BlockSpec((1, TG, width), lambda i, t: (i, t, 0))

    def const(shape):
        return pl.BlockSpec(shape, lambda i, t: (0, 0))

    return pl.pallas_call(
        _gla_kernel,
        out_shape=jax.ShapeDtypeStruct((b, s, GLA_VW), BF16),
        grid=(b, s // TG),
        in_specs=[rows(GLA_KW), rows(GLA_KW), rows(GLA_VW), rows(GLA_VW), rows(SMALL_W),
                  const((SMALL_W, GLA_KW)), const((1, GLA_KW)), const((1, GLA_DV))],
        out_specs=rows(GLA_VW),
        scratch_shapes=[pltpu.VMEM((GLA_DV, GLA_KW), F32)],
        compiler_params=pltpu.CompilerParams(dimension_semantics=("arbitrary", "arbitrary")),
        name="gla",
    )(gq, gk, gv, go, small, w2, b2, gn)


def _out_kernel(x_ref, fox_ref, gla_ref, wo_ref, g_ref, wg_ref, wu_ref, wd_ref, gf_ref, o_ref):
    x = x_ref[...]
    x = x + jnp.dot(fox_ref[...], wo_ref[:FOX_WIDTH, :], preferred_element_type=F32)
    x = x + jnp.dot(gla_ref[...], wo_ref[FOX_WIDTH:, :], preferred_element_type=F32)
    h = _rms(x, g_ref[...]).astype(BF16)
    x = x + 0.5 * _swiglu(h, wg_ref, wu_ref, wd_ref)
    o_ref[...] = _rms(x, gf_ref[...])


def _out_ffn2(x1, fox, gla, wo, g, wg, wu, wd, gf):
    n = x1.shape[0]

    def rows(width):
        return pl.BlockSpec((TM_FFN, width), lambda i: (i, 0))

    return pl.pallas_call(
        _out_kernel,
        out_shape=jax.ShapeDtypeStruct((n, D_MODEL), F32),
        grid=(n // TM_FFN,),
        in_specs=[rows(D_MODEL), rows(FOX_WIDTH), rows(GLA_VW),
                  _const_spec((FOX_WIDTH + GLA_VW, D_MODEL)), _const_spec((1, D_MODEL)),
                  _const_spec((D_MODEL, D_FF)), _const_spec((D_MODEL, D_FF)),
                  _const_spec((D_FF, D_MODEL)), _const_spec((1, D_MODEL))],
        out_specs=rows(D_MODEL),
        compiler_params=pltpu.CompilerParams(
            dimension_semantics=("arbitrary",), vmem_limit_bytes=VMEM_LIMIT),
        name="out_ffn2",
    )(x1, fox, gla, wo, g, wg, wu, wd, gf)


def _pack_w_in(w_in):
    widths = (FOX_WIDTH, FOX_WIDTH, FOX_WIDTH, FOX_HEADS, GLA_KW, GLA_KW, GLA_VW, GLA_RANK, GLA_VW)
    offs = [0]
    for wd in widths:
        offs.append(offs[-1] + wd)
    fq, fk, fv, ff, gq, gk, gv, glow, gout = (w_in[:, offs[i]:offs[i + 1]] for i in range(9))
    pad = jnp.zeros((D_MODEL, SMALL_W - _FF_LANES - GLA_RANK), w_in.dtype)
    packed = jnp.concatenate(
        [fq * (_LOG2E * FOX_DH ** -0.5), fk, fv, gq * (GLA_DK ** -0.5), gk, gv, gout,
         jnp.tile(ff, (1, _FF_LANES // FOX_HEADS)), glow, pad], axis=1)
    return packed.astype(BF16)


def kernel(x, ffn1_norm, ffn1_w_gate, ffn1_w_up, ffn1_w_down, mix_norm, w_in, fox_forget_bias,
           gla_w_gate_up, gla_gate_bias, gla_out_norm, w_out, ffn2_norm, ffn2_w_gate, ffn2_w_up,
           ffn2_w_down, final_norm):
    b, s, d = x.shape
    n = b * s
    assert ffn1_norm.shape[0] == 1, "the final norm is fused into the single layer's last call"
    x = x.reshape(n, d)
    for l in range(1):
        x1 = _ffn1(x, ffn1_norm[l].reshape(1, d), ffn1_w_gate[l].astype(BF16),
                   ffn1_w_up[l].astype(BF16), ffn1_w_down[l].astype(BF16))

        fq, fk, fv, gq, gk, gv, go, small = _in_proj(
            x1, mix_norm[l].reshape(1, d), _pack_w_in(w_in[l]))

        bias = jnp.zeros((1, SMALL_W), F32).at[0, :_FF_LANES].set(
            jnp.tile(fox_forget_bias[l], _FF_LANES // FOX_HEADS))
        ka, qa = _fox_decay(small.reshape(b, s, SMALL_W), bias)
        fox = _fox_attn(fq.reshape(b, s, -1), fk.reshape(b, s, -1), fv.reshape(b, s, -1), qa, ka)

        w2 = jnp.zeros((SMALL_W, GLA_KW), F32).at[_FF_LANES:_FF_LANES + GLA_RANK].set(
            gla_w_gate_up[l]).astype(BF16)
        gla = _gla(gq.reshape(b, s, -1), gk.reshape(b, s, -1), gv.reshape(b, s, -1),
                   go.reshape(b, s, -1), small.reshape(b, s, SMALL_W), w2,
                   gla_gate_bias[l].reshape(1, GLA_KW), gla_out_norm[l].reshape(1, GLA_DV))

        x = _out_ffn2(x1, fox.reshape(n, -1), gla.reshape(n, -1), w_out[l].astype(BF16),
                      ffn2_norm[l].reshape(1, d), ffn2_w_gate[l].astype(BF16),
                      ffn2_w_up[l].astype(BF16), ffn2_w_down[l].astype(BF16),
                      final_norm.reshape(1, d))
    return x.reshape(b, s, d)
```

```python
import functools

import jax
import jax.numpy as jnp
from jax import lax
from jax.experimental import pallas as pl
from jax.experimental.pallas import tpu as pltpu

F32 = jnp.float32
BF16 = jnp.bfloat16

D_MODEL = 1024
D_FF = 2816
EPS = 1e-6
CHUNK = 64

FOX_HEADS = 8
FOX_DH = 64
FOX_WIDTH = FOX_HEADS * FOX_DH
GLA_HEADS = 4
GLA_DK = 64
GLA_DV = 128
GLA_KW = GLA_HEADS * GLA_DK
GLA_VW = GLA_HEADS * GLA_DV
GLA_RANK = 16
GLA_TEMP = 16.0

LANES = 128
SMALL_W = LANES
_C_FQ = 0
_C_FK = _C_FQ + FOX_WIDTH
_C_FV = _C_FK + FOX_WIDTH
_C_GQ = _C_FV + FOX_WIDTH
_C_GK = _C_GQ + GLA_KW
_C_GV = _C_GK + GLA_KW
_C_GO = _C_GV + GLA_VW
_C_SM = _C_GO + GLA_VW
PROJ_W = _C_SM + SMALL_W

TM_FFN = 1024
FF_BOUNDS = (0, 1024, 2048, D_FF)
TM_PROJ = 1024
TQ = 256
TG = 1024
VMEM_LIMIT = 56 * 1024 * 1024


def _rms(x, g):
    return x * lax.rsqrt(jnp.mean(x * x, axis=-1, keepdims=True) + EPS) * g


def _log_sigmoid(z):
    return jnp.minimum(z, 0.0) - jnp.log(1.0 + jnp.exp(-jnp.abs(z)))


def _swiglu(h, wg_ref, wu_ref, wd_ref):
    y = None
    for c0, c1 in zip(FF_BOUNDS[:-1], FF_BOUNDS[1:]):
        gate = jnp.dot(h, wg_ref[:, c0:c1], preferred_element_type=F32)
        up = jnp.dot(h, wu_ref[:, c0:c1], preferred_element_type=F32)
        act = (gate * jax.nn.sigmoid(gate) * up).astype(BF16)
        part = jnp.dot(act, wd_ref[c0:c1, :], preferred_element_type=F32)
        y = part if y is None else y + part
    return y


def _split3(x):
    hi = x.astype(BF16)
    r1 = x - hi.astype(F32)
    mid = r1.astype(BF16)
    lo = (r1 - mid.astype(F32)).astype(BF16)
    return hi, mid, lo


def _dot3(sel, x):
    hi, mid, lo = _split3(x)
    return (jnp.dot(sel, hi, preferred_element_type=F32)
            + jnp.dot(sel, mid, preferred_element_type=F32)
            + jnp.dot(sel, lo, preferred_element_type=F32))


def _const_spec(shape):
    n = len(shape)
    return pl.BlockSpec(shape, lambda *_: (0,) * n, pipeline_mode=pl.Buffered(1))


def _ffn1_kernel(x_ref, g_ref, wg_ref, wu_ref, wd_ref, o_ref):
    x = x_ref[...]
    h = _rms(x, g_ref[...]).astype(BF16)
    o_ref[...] = x + 0.5 * _swiglu(h, wg_ref, wu_ref, wd_ref)


def _ffn1(x, g, wg, wu, wd):
    n = x.shape[0]
    row = pl.BlockSpec((TM_FFN, D_MODEL), lambda i: (i, 0))
    return pl.pallas_call(
        _ffn1_kernel,
        out_shape=jax.ShapeDtypeStruct((n, D_MODEL), F32),
        grid=(n // TM_FFN,),
        in_specs=[row, _const_spec((1, D_MODEL)), _const_spec((D_MODEL, D_FF)),
                  _const_spec((D_MODEL, D_FF)), _const_spec((D_FF, D_MODEL))],
        out_specs=row,
        compiler_params=pltpu.CompilerParams(
            dimension_semantics=("arbitrary",), vmem_limit_bytes=VMEM_LIMIT),
        name="ffn1",
    )(x, g, wg, wu, wd)


def _proj_kernel(x_ref, g_ref, w_ref, fq_ref, fk_ref, fv_ref, gq_ref, gk_ref, gv_ref,
                 go_ref, sm_ref):
    h = _rms(x_ref[...], g_ref[...]).astype(BF16)

    def proj(c0, width):
        return jnp.dot(h, w_ref[:, c0:c0 + width], preferred_element_type=F32)

    fq_ref[...] = proj(_C_FQ, FOX_WIDTH).astype(BF16)
    fk_ref[...] = proj(_C_FK, FOX_WIDTH).astype(BF16)
    fv_ref[...] = proj(_C_FV, FOX_WIDTH).astype(BF16)
    gq_ref[...] = proj(_C_GQ, GLA_KW).astype(BF16)
    gk_ref[...] = proj(_C_GK, GLA_KW)
    gv_ref[...] = proj(_C_GV, GLA_VW).astype(BF16)
    go_ref[...] = proj(_C_GO, GLA_VW)
    sm_ref[...] = proj(_C_SM, SMALL_W)


def _in_proj(x1, g, w):
    n = x1.shape[0]

    def rows(width):
        return pl.BlockSpec((TM_PROJ, width), lambda i: (i, 0))

    widths = (FOX_WIDTH, FOX_WIDTH, FOX_WIDTH, GLA_KW, GLA_KW, GLA_VW, GLA_VW, SMALL_W)
    dtypes = (BF16, BF16, BF16, BF16, F32, BF16, F32, F32)
    return pl.pallas_call(
        _proj_kernel,
        out_shape=[jax.ShapeDtypeStruct((n, wd), dt) for wd, dt in zip(widths, dtypes)],
        grid=(n // TM_PROJ,),
        in_specs=[rows(D_MODEL), _const_spec((1, D_MODEL)), _const_spec((D_MODEL, PROJ_W))],
        out_specs=[rows(wd) for wd in widths],
        compiler_params=pltpu.CompilerParams(
            dimension_semantics=("arbitrary",), vmem_limit_bytes=VMEM_LIMIT),
        name="in_proj",
    )(x1, g, w)


_DECAY_BLK = 256
_N_TERMS = 3
_BIAS_LANES = _N_TERMS * FOX_HEADS
_FF_LANES = 2 * _BIAS_LANES
_LOG2E = 1.4426950408889634


def _decay_kernel(sm_ref, b_ref, ka_ref, qa_ref):
    seq = sm_ref.shape[1]
    r = lax.broadcasted_iota(jnp.int32, (_DECAY_BLK, _DECAY_BLK), 0)
    c = lax.broadcasted_iota(jnp.int32, (_DECAY_BLK, _DECAY_BLK), 1)
    tri = (r >= c).astype(BF16)
    lane = lax.broadcasted_iota(jnp.int32, (_DECAY_BLK, SMALL_W), 1)
    term = (lane >> 3) % _N_TERMS
    carry = jnp.zeros((1, SMALL_W), F32)
    for blk in range(seq // _DECAY_BLK):
        rows = slice(blk * _DECAY_BLK, (blk + 1) * _DECAY_BLK)
        log_f = _log_sigmoid(sm_ref[0, rows, :] + b_ref[...])
        log_f = jnp.where(lane < _FF_LANES, log_f, 0.0)
        cum = _dot3(tri, log_f) + carry
        carry = cum[_DECAY_BLK - 1:_DECAY_BLK, :]
        hi, mid, lo = _split3(cum * _LOG2E)
        f_term = jnp.where(term == 0, hi.astype(F32),
                           jnp.where(term == 1, mid.astype(F32), lo.astype(F32)))
        one = jnp.ones_like(f_term)
        zero = jnp.zeros_like(f_term)
        ka_ref[0, rows, :] = jnp.where(
            lane < _BIAS_LANES, -f_term, jnp.where(lane < _FF_LANES, one, zero)).astype(BF16)
        qa_ref[0, rows, :] = jnp.where(
            lane < _BIAS_LANES, one, jnp.where(lane < _FF_LANES, f_term, zero)).astype(BF16)


def _fox_decay(small, bias):
    b, s, _ = small.shape
    blk = pl.BlockSpec((1, s, SMALL_W), lambda i: (i, 0, 0))
    return pl.pallas_call(
        _decay_kernel,
        out_shape=[jax.ShapeDtypeStruct((b, s, SMALL_W), BF16)] * 2,
        grid=(b,),
        in_specs=[blk, pl.BlockSpec((1, SMALL_W), lambda i: (0, 0))],
        out_specs=[blk, blk],
        compiler_params=pltpu.CompilerParams(dimension_semantics=("arbitrary",)),
        name="fox_decay",
    )(small, bias)


_NT = (((1,), (1,)), ((), ()))
_LOOKAHEAD = 2


def _attn_kernel(q_ref, k_ref, v_ref, qa_ref, ka_ref, o_ref):
    hp = pl.program_id(1)
    seq = q_ref.shape[1]
    lane = lax.broadcasted_iota(jnp.int32, (TQ, LANES), 1)
    k_idx = lax.broadcasted_iota(jnp.int32, (TQ, TQ), 0)
    q_idx = lax.broadcasted_iota(jnp.int32, (TQ, TQ), 1)
    causal = k_idx <= q_idx

    k_aug = jnp.concatenate([k_ref[0], ka_ref[0]], axis=1)
    v_t = v_ref[0].astype(F32).T.astype(BF16)

    def logits(qi):
        q0 = qi * TQ
        klen = q0 + TQ
        q = q_ref[0, q0:klen, :]
        qa = qa_ref[0, q0:klen, :]
        out = []
        for hh in range(2):
            in_head = (lane >= hh * FOX_DH) & (lane < (hh + 1) * FOX_DH)
            bias_lane = ((lane & (FOX_HEADS - 1)) == 2 * hp + hh) & (lane < _FF_LANES)
            q_aug = jnp.concatenate([jnp.where(in_head, q, jnp.zeros_like(q)),
                                     jnp.where(bias_lane, qa, jnp.zeros_like(qa))], axis=1)
            out.append(lax.dot_general(k_aug[:klen], q_aug, _NT, preferred_element_type=F32))
        return out

    def finish(qi, s_both):
        q0 = qi * TQ
        klen = q0 + TQ
        outs = []
        for hh in range(2):
            s_t = s_both[hh]
            s_diag = jnp.where(causal, s_t[q0:], -jnp.inf)
            m = jnp.max(s_diag, axis=0, keepdims=True)
            if qi > 0:
                m = jnp.maximum(m, jnp.max(s_t[:q0], axis=0, keepdims=True))
            p_t = jnp.exp2(s_diag - m)
            if qi > 0:
                p_t = jnp.concatenate([jnp.exp2(s_t[:q0] - m), p_t], axis=0)
            l = jnp.sum(p_t, axis=0, keepdims=True)
            o_t = jnp.dot(v_t[hh * FOX_DH:(hh + 1) * FOX_DH, :klen], p_t.astype(BF16),
                          preferred_element_type=F32)
            outs.append(o_t * (1.0 / l))
        o_ref[0, q0:klen, :] = jnp.concatenate(outs, axis=0).T.astype(o_ref.dtype)

    n_blocks = seq // TQ
    order = list(range(1, n_blocks, 2)) + list(range(n_blocks - 2 + n_blocks % 2, -1, -2))
    ready = {qi: logits(qi) for qi in order[:_LOOKAHEAD]}
    for pos, qi in enumerate(order):
        if pos + _LOOKAHEAD < len(order):
            ahead = order[pos + _LOOKAHEAD]
            ready[ahead] = logits(ahead)
        finish(qi, ready.pop(qi))


def _fox_attn(fq, fk, fv, qa, ka):
    b, s, _ = fq.shape
    qkv = pl.BlockSpec((1, s, LANES), lambda i, h: (i, 0, h))
    aug = pl.BlockSpec((1, s, SMALL_W), lambda i, h: (i, 0, 0))
    return pl.pallas_call(
        _attn_kernel,
        out_shape=jax.ShapeDtypeStruct((b, s, FOX_WIDTH), BF16),
        grid=(b, FOX_HEADS // 2),
        in_specs=[qkv, qkv, qkv, aug, aug],
        out_specs=qkv,
        compiler_params=pltpu.CompilerParams(
            dimension_semantics=("arbitrary", "arbitrary"), vmem_limit_bytes=VMEM_LIMIT),
        name="fox_attn",
    )(fq, fk, fv, qa, ka)


_TN = (((0,), (0,)), ((), ()))
_N_CHUNK = TG // CHUNK
_CHUNK_SHIFT = CHUNK.bit_length() - 1
_DK_SHIFT = GLA_DK.bit_length() - 1


def _gla_kernel(gq_ref, gk_ref, gv_ref, go_ref, sm_ref, w2_ref, b2_ref, gn_ref, o_ref, st_ref):
    @pl.when(pl.program_id(1) == 0)
    def _():
        st_ref[...] = jnp.zeros_like(st_ref)

    z = jnp.dot(sm_ref[0].astype(BF16), w2_ref[...], preferred_element_type=F32) + b2_ref[...]
    log_a = _log_sigmoid(z) / GLA_TEMP

    r = lax.broadcasted_iota(jnp.int32, (CHUNK, CHUNK), 0)
    c = lax.broadcasted_iota(jnp.int32, (CHUNK, CHUNK), 1)
    tri = (r >= c).astype(BF16)
    terms = jnp.concatenate(_split3(log_a), axis=1)
    chunks = [slice(n * CHUNK, (n + 1) * CHUNK) for n in range(_N_CHUNK)]
    lane_k = lax.broadcasted_iota(jnp.int32, (CHUNK, GLA_KW), 1) >> _DK_SHIFT

    deltas, decays = [], []
    for rows in chunks:
        cum3 = jnp.dot(tri, terms[rows], preferred_element_type=F32)
        g_cum = cum3[:, :GLA_KW] + cum3[:, GLA_KW:2 * GLA_KW] + cum3[:, 2 * GLA_KW:]
        g_tot = g_cum[CHUNK - 1:CHUNK, :]
        k_dec = (gk_ref[0, rows, :] * jnp.exp(g_tot - g_cum)).astype(BF16)
        delta = None
        for h in range(GLA_HEADS):
            part = lax.dot_general(gv_ref[0, rows, h * GLA_DV:(h + 1) * GLA_DV],
                                   jnp.where(lane_k == h, k_dec, jnp.zeros_like(k_dec)), _TN,
                                   preferred_element_type=F32)
            delta = part if delta is None else delta + part
        deltas.append(delta)
        decays.append(jnp.exp(g_tot))

    state = st_ref[...]
    states = []
    for delta, decay in zip(deltas, decays):
        state = decay * state + delta
        states.append(state.astype(BF16))
    st_ref[...] = state

    g_norm = gn_ref[...]
    for rows, st in zip(chunks, states):
        qc = gq_ref[0, rows, :]
        q_stack = jnp.concatenate(
            [jnp.where(lane_k == h, qc, jnp.zeros_like(qc)) for h in range(GLA_HEADS)], axis=0)
        o_all = lax.dot_general(q_stack, st, _NT, preferred_element_type=F32)
        for h in range(GLA_HEADS):
            o = _rms(o_all[h * CHUNK:(h + 1) * CHUNK, :], g_norm)
            gate = go_ref[0, rows, h * GLA_DV:(h + 1) * GLA_DV]
            o_ref[0, rows, h * GLA_DV:(h + 1) * GLA_DV] = (
                o * (gate * jax.nn.sigmoid(gate))).astype(o_ref.dtype)


def _gla(gq, gk, gv, go, small, w2, b2, gn):
    b, s, _ = gq.shape

    def rows(width):
        return pl.BlockSpec((1, TG, width), lambda i, t: (i, t, 0))

    def const(shape):
        return pl.BlockSpec(shape, lambda i, t: (0, 0))

    return pl.pallas_call(
        _gla_kernel,
        out_shape=jax.ShapeDtypeStruct((b, s, GLA_VW), BF16),
        grid=(b, s // TG),
        in_specs=[rows(GLA_KW), rows(GLA_KW), rows(GLA_VW), rows(GLA_VW), rows(SMALL_W),
                  const((SMALL_W, GLA_KW)), const((1, GLA_KW)), const((1, GLA_DV))],
        out_specs=rows(GLA_VW),
        scratch_shapes=[pltpu.VMEM((GLA_DV, GLA_KW), F32)],
        compiler_params=pltpu.CompilerParams(dimension_semantics=("arbitrary", "arbitrary")),
        name="gla",
    )(gq, gk, gv, go, small, w2, b2, gn)


def _out_kernel(x_ref, fox_ref, gla_ref, wo_ref, g_ref, wg_ref, wu_ref, wd_ref, gf_ref, o_ref):
    x = x_ref[...]
    x = x + jnp.dot(fox_ref[...], wo_ref[:FOX_WIDTH, :], preferred_element_type=F32)
    x = x + jnp.dot(gla_ref[...], wo_ref[FOX_WIDTH:, :], preferred_element_type=F32)
    h = _rms(x, g_ref[...]).astype(BF16)
    x = x + 0.5 * _swiglu(h, wg_ref, wu_ref, wd_ref)
    o_ref[...] = _rms(x, gf_ref[...])


def _out_ffn2(x1, fox, gla, wo, g, wg, wu, wd, gf):
    n = x1.shape[0]

    def rows(width):
        return pl.BlockSpec((TM_FFN, width), lambda i: (i, 0))

    return pl.pallas_call(
        _out_kernel,
        out_shape=jax.ShapeDtypeStruct((n, D_MODEL), F32),
        grid=(n // TM_FFN,),
        in_specs=[rows(D_MODEL), rows(FOX_WIDTH), rows(GLA_VW),
                  _const_spec((FOX_WIDTH + GLA_VW, D_MODEL)), _const_spec((1, D_MODEL)),
                  _const_spec((D_MODEL, D_FF)), _const_spec((D_MODEL, D_FF)),
                  _const_spec((D_FF, D_MODEL)), _const_spec((1, D_MODEL))],
        out_specs=rows(D_MODEL),
        compiler_params=pltpu.CompilerParams(
            dimension_semantics=("arbitrary",), vmem_limit_bytes=VMEM_LIMIT),
        name="out_ffn2",
    )(x1, fox, gla, wo, g, wg, wu, wd, gf)


def _pack_w_in(w_in):
    widths = (FOX_WIDTH, FOX_WIDTH, FOX_WIDTH, FOX_HEADS, GLA_KW, GLA_KW, GLA_VW, GLA_RANK, GLA_VW)
    offs = [0]
    for wd in widths:
        offs.append(offs[-1] + wd)
    fq, fk, fv, ff, gq, gk, gv, glow, gout = (w_in[:, offs[i]:offs[i + 1]] for i in range(9))
    pad = jnp.zeros((D_MODEL, SMALL_W - _FF_LANES - GLA_RANK), w_in.dtype)
    packed = jnp.concatenate(
        [fq * (_LOG2E * FOX_DH ** -0.5), fk, fv, gq * (GLA_DK ** -0.5), gk, gv, gout,
         jnp.tile(ff, (1, _FF_LANES // FOX_HEADS)), glow, pad], axis=1)
    return packed.astype(BF16)


def kernel(x, ffn1_norm, ffn1_w_gate, ffn1_w_up, ffn1_w_down, mix_norm, w_in, fox_forget_bias,
           gla_w_gate_up, gla_gate_bias, gla_out_norm, w_out, ffn2_norm, ffn2_w_gate, ffn2_w_up,
           ffn2_w_down, final_norm):
    b, s, d = x.shape
    n = b * s
    assert ffn1_norm.shape[0] == 1, "the final norm is fused into the single layer's last call"
    x = x.reshape(n, d)
    for l in range(1):
        x1 = _ffn1(x, ffn1_norm[l].reshape(1, d), ffn1_w_gate[l].astype(BF16),
                   ffn1_w_up[l].astype(BF16), ffn1_w_down[l].astype(BF16))

        fq, fk, fv, gq, gk, gv, go, small = _in_proj(
            x1, mix_norm[l].reshape(1, d), _pack_w_in(w_in[l]))

        bias = jnp.zeros((1, SMALL_W), F32).at[0, :_FF_LANES].set(
            jnp.tile(fox_forget_bias[l], _FF_LANES // FOX_HEADS))
        ka, qa = _fox_decay(small.reshape(b, s, SMALL_W), bias)
        fox = _fox_attn(fq.reshape(b, s, -1), fk.reshape(b, s, -1), fv.reshape(b, s, -1), qa, ka)

        w2 = jnp.zeros((SMALL_W, GLA_KW), F32).at[_FF_LANES:_FF_LANES + GLA_RANK].set(
            gla_w_gate_up[l]).astype(BF16)
        gla = _gla(gq.reshape(b, s, -1), gk.reshape(b, s, -1), gv.reshape(b, s, -1),
                   go.reshape(b, s, -1), small.reshape(b, s, SMALL_W), w2,
                   gla_gate_bias[l].reshape(1, GLA_KW), gla_out_norm[l].reshape(1, GLA_DV))

        x = _out_ffn2(x1, fox.reshape(n, -1), gla.reshape(n, -1), w_out[l].astype(BF16),
                      ffn2_norm[l].reshape(1, d), ffn2_w_gate[l].astype(BF16),
                      ffn2_w_up[l].astype(BF16), ffn2_w_down[l].astype(BF16),
                      final_norm.reshape(1, d))
    return x.reshape(b, s, d)
```

```python
import jax
import jax.numpy as jnp
from jax import lax
from jax.experimental import pallas as pl
from jax.experimental.pallas import tpu as pltpu

F32 = jnp.float32
BF16 = jnp.bfloat16

D_MODEL = 1024
D_FF = 2816
EPS = 1e-6
CHUNK = 64

FOX_HEADS = 8
FOX_DH = 64
FOX_WIDTH = FOX_HEADS * FOX_DH
GLA_HEADS = 4
GLA_DK = 64
GLA_DV = 128
GLA_KW = GLA_HEADS * GLA_DK
GLA_VW = GLA_HEADS * GLA_DV
GLA_RANK = 16
GLA_TEMP = 16.0

LANES = 128
SMALL_W = LANES
_C_FQ = 0
_C_FK = _C_FQ + FOX_WIDTH
_C_FV = _C_FK + FOX_WIDTH
_C_GQ = _C_FV + FOX_WIDTH
_C_GK = _C_GQ + GLA_KW
_C_GV = _C_GK + GLA_KW
_C_GO = _C_GV + GLA_VW
_C_SM = _C_GO + GLA_VW
PROJ_W = _C_SM + SMALL_W

TM_FFN = 1024
FF_BOUNDS = (0, 1024, 2048, D_FF)
TM_PROJ = 1024
TQ = 256
VMEM_LIMIT = 56 * 1024 * 1024


def _rms(x, g):
    return x * lax.rsqrt(jnp.mean(x * x, axis=-1, keepdims=True) + EPS) * g


def _log_sigmoid(z):
    return jnp.minimum(z, 0.0) - jnp.log(1.0 + jnp.exp(-jnp.abs(z)))


def _swiglu(h, wg_ref, wu_ref, wd_ref):
    y = None
    for c0, c1 in zip(FF_BOUNDS[:-1], FF_BOUNDS[1:]):
        gate = jnp.dot(h, wg_ref[:, c0:c1], preferred_element_type=F32)
        up = jnp.dot(h, wu_ref[:, c0:c1], preferred_element_type=F32)
        act = (gate * jax.nn.sigmoid(gate) * up).astype(BF16)
        part = jnp.dot(act, wd_ref[c0:c1, :], preferred_element_type=F32)
        y = part if y is None else y + part
    return y


def _split3(x):
    hi = x.astype(BF16)
    r1 = x - hi.astype(F32)
    mid = r1.astype(BF16)
    lo = (r1 - mid.astype(F32)).astype(BF16)
    return hi, mid, lo


def _dot3(sel, x):
    hi, mid, lo = _split3(x)
    return (jnp.dot(sel, hi, preferred_element_type=F32)
            + jnp.dot(sel, mid, preferred_element_type=F32)
            + jnp.dot(sel, lo, preferred_element_type=F32))


def _const_spec(shape):
    n = len(shape)
    return pl.BlockSpec(shape, lambda *_: (0,) * n, pipeline_mode=pl.Buffered(1))


def _ffn1_kernel(x_ref, g_ref, wg_ref, wu_ref, wd_ref, o_ref):
    x = x_ref[...]
    h = _rms(x, g_ref[...]).astype(BF16)
    o_ref[...] = x + 0.5 * _swiglu(h, wg_ref, wu_ref, wd_ref)


def _ffn1(x, g, wg, wu, wd):
    n = x.shape[0]
    row = pl.BlockSpec((TM_FFN, D_MODEL), lambda i: (i, 0))
    return pl.pallas_call(
        _ffn1_kernel,
        out_shape=jax.ShapeDtypeStruct((n, D_MODEL), F32),
        grid=(n // TM_FFN,),
        in_specs=[row, _const_spec((1, D_MODEL)), _const_spec((D_MODEL, D_FF)),
                  _const_spec((D_MODEL, D_FF)), _const_spec((D_FF, D_MODEL))],
        out_specs=row,
        compiler_params=pltpu.CompilerParams(
            dimension_semantics=("arbitrary",), vmem_limit_bytes=VMEM_LIMIT),
        name="ffn1",
    )(x, g, wg, wu, wd)


_DECAY_BLK = 256
_N_TERMS = 3
_BIAS_LANES = _N_TERMS * FOX_HEADS
_FF_LANES = 2 * _BIAS_LANES
_LOG2E = 1.4426950408889634
_NT = (((1,), (1,)), ((), ()))
_TN = (((0,), (0,)), ((), ()))
_N_CHUNK = TM_PROJ // CHUNK
_DK_SHIFT = GLA_DK.bit_length() - 1


def _decay_operands(small, bias, offset):
    r = lax.broadcasted_iota(jnp.int32, (_DECAY_BLK, _DECAY_BLK), 0)
    c = lax.broadcasted_iota(jnp.int32, (_DECAY_BLK, _DECAY_BLK), 1)
    tri = (r >= c).astype(BF16)
    lane = lax.broadcasted_iota(jnp.int32, (_DECAY_BLK, SMALL_W), 1)
    term = (lane >> 3) % _N_TERMS
    blocks = [slice(b0, b0 + _DECAY_BLK) for b0 in range(0, small.shape[0], _DECAY_BLK)]
    local = []
    for rows in blocks:
        log_f = _log_sigmoid(small[rows] + bias)
        local.append(_dot3(tri, jnp.where(lane < _FF_LANES, log_f, 0.0)))
    ka, qa = [], []
    for cum_local in local:
        cum = cum_local + offset
        offset = cum[_DECAY_BLK - 1:_DECAY_BLK, :]
        hi, mid, lo = _split3(cum * _LOG2E)
        f_term = jnp.where(term == 0, hi.astype(F32),
                           jnp.where(term == 1, mid.astype(F32), lo.astype(F32)))
        one = jnp.ones_like(f_term)
        zero = jnp.zeros_like(f_term)
        ka.append(jnp.where(lane < _BIAS_LANES, -f_term,
                            jnp.where(lane < _FF_LANES, one, zero)).astype(BF16))
        qa.append(jnp.where(lane < _BIAS_LANES, one,
                            jnp.where(lane < _FF_LANES, f_term, zero)).astype(BF16))
    return jnp.concatenate(ka, axis=0), jnp.concatenate(qa, axis=0), offset


def _gla_tile(gq, gk, gv, go, small, w2, b2, g_norm, state, o_ref):
    z = jnp.dot(small.astype(BF16), w2, preferred_element_type=F32) + b2
    log_a = _log_sigmoid(z) / GLA_TEMP

    r = lax.broadcasted_iota(jnp.int32, (CHUNK, CHUNK), 0)
    c = lax.broadcasted_iota(jnp.int32, (CHUNK, CHUNK), 1)
    tri = (r >= c).astype(BF16)
    terms = jnp.concatenate(_split3(log_a), axis=1)
    chunks = [slice(n * CHUNK, (n + 1) * CHUNK) for n in range(_N_CHUNK)]
    lane_k = lax.broadcasted_iota(jnp.int32, (CHUNK, GLA_KW), 1) >> _DK_SHIFT

    deltas, decays = [], []
    for rows in chunks:
        cum3 = jnp.dot(tri, terms[rows], preferred_element_type=F32)
        g_cum = cum3[:, :GLA_KW] + cum3[:, GLA_KW:2 * GLA_KW] + cum3[:, 2 * GLA_KW:]
        g_tot = g_cum[CHUNK - 1:CHUNK, :]
        k_dec = (gk[rows] * jnp.exp(g_tot - g_cum)).astype(BF16)
        delta = None
        for h in range(GLA_HEADS):
            part = lax.dot_general(gv[rows, h * GLA_DV:(h + 1) * GLA_DV],
                                   jnp.where(lane_k == h, k_dec, jnp.zeros_like(k_dec)), _TN,
                                   preferred_element_type=F32)
            delta = part if delta is None else delta + part
        deltas.append(delta)
        decays.append(jnp.exp(g_tot))

    states = []
    for delta, decay in zip(deltas, decays):
        state = decay * state + delta
        states.append(state.astype(BF16))

    for rows, st in zip(chunks, states):
        qc = gq[rows]
        q_stack = jnp.concatenate(
            [jnp.where(lane_k == h, qc, jnp.zeros_like(qc)) for h in range(GLA_HEADS)], axis=0)
        o_all = lax.dot_general(q_stack, st, _NT, preferred_element_type=F32)
        for h in range(GLA_HEADS):
            o = _rms(o_all[h * CHUNK:(h + 1) * CHUNK, :], g_norm)
            gate = go[rows, h * GLA_DV:(h + 1) * GLA_DV]
            o_ref[0, rows, h * GLA_DV:(h + 1) * GLA_DV] = (
                o * (gate * jax.nn.sigmoid(gate))).astype(o_ref.dtype)
    return state


def _mix_in_kernel(x_ref, g_ref, w_ref, fb_ref, w2_ref, b2_ref, gn_ref,
                   fq_ref, fk_ref, fv_ref, ka_ref, qa_ref, gla_ref, st_ref, off_ref):
    @pl.when(pl.program_id(1) == 0)
    def _():
        st_ref[...] = jnp.zeros_like(st_ref)
        off_ref[...] = jnp.zeros_like(off_ref)

    h = _rms(x_ref[0], g_ref[...]).astype(BF16)

    def proj(c0, width):
        return jnp.dot(h, w_ref[:, c0:c0 + width], preferred_element_type=F32)

    small = proj(_C_SM, SMALL_W)
    gk = proj(_C_GK, GLA_KW)
    gv = proj(_C_GV, GLA_VW).astype(BF16)
    gq = proj(_C_GQ, GLA_KW).astype(BF16)
    go = proj(_C_GO, GLA_VW)
    st_ref[...] = _gla_tile(gq, gk, gv, go, small, w2_ref[...], b2_ref[...], gn_ref[...],
                            st_ref[...], gla_ref)
    ka, qa, off_ref[...] = _decay_operands(small, fb_ref[...], off_ref[...])
    ka_ref[0] = ka
    qa_ref[0] = qa
    fq_ref[0] = proj(_C_FQ, FOX_WIDTH).astype(BF16)
    fk_ref[0] = proj(_C_FK, FOX_WIDTH).astype(BF16)
    fv_ref[0] = proj(_C_FV, FOX_WIDTH).astype(BF16)


def _mix_in(x1, g, w, fox_bias, w2, b2, gn):
    b, s, _ = x1.shape

    def rows(width):
        return pl.BlockSpec((1, TM_PROJ, width), lambda i, t: (i, t, 0))

    widths = (FOX_WIDTH, FOX_WIDTH, FOX_WIDTH, SMALL_W, SMALL_W, GLA_VW)
    return pl.pallas_call(
        _mix_in_kernel,
        out_shape=[jax.ShapeDtypeStruct((b, s, wd), BF16) for wd in widths],
        grid=(b, s // TM_PROJ),
        in_specs=[rows(D_MODEL), _const_spec((1, D_MODEL)), _const_spec((D_MODEL, PROJ_W)),
                  _const_spec((1, SMALL_W)), _const_spec((SMALL_W, GLA_KW)),
                  _const_spec((1, GLA_KW)), _const_spec((1, GLA_DV))],
        out_specs=[rows(wd) for wd in widths],
        scratch_shapes=[pltpu.VMEM((GLA_DV, GLA_KW), F32), pltpu.VMEM((1, SMALL_W), F32)],
        compiler_params=pltpu.CompilerParams(
            dimension_semantics=("arbitrary", "arbitrary"), vmem_limit_bytes=VMEM_LIMIT),
        name="mix_in",
    )(x1, g, w, fox_bias, w2, b2, gn)


_LOOKAHEAD = 2


def _attn_kernel(q_ref, k_ref, v_ref, qa_ref, ka_ref, o_ref):
    hp = pl.program_id(1)
    seq = q_ref.shape[1]
    lane = lax.broadcasted_iota(jnp.int32, (TQ, LANES), 1)
    k_idx = lax.broadcasted_iota(jnp.int32, (TQ, TQ), 0)
    q_idx = lax.broadcasted_iota(jnp.int32, (TQ, TQ), 1)
    causal = k_idx <= q_idx

    k_aug = jnp.concatenate([k_ref[0], ka_ref[0]], axis=1)
    v_t = v_ref[0].astype(F32).T.astype(BF16)

    def logits(qi):
        q0 = qi * TQ
        klen = q0 + TQ
        q = q_ref[0, q0:klen, :]
        qa = qa_ref[0, q0:klen, :]
        out = []
        for hh in range(2):
            in_head = (lane >= hh * FOX_DH) & (lane < (hh + 1) * FOX_DH)
            bias_lane = ((lane & (FOX_HEADS - 1)) == 2 * hp + hh) & (lane < _FF_LANES)
            q_aug = jnp.concatenate([jnp.where(in_head, q, jnp.zeros_like(q)),
                                     jnp.where(bias_lane, qa, jnp.zeros_like(qa))], axis=1)
            out.append(lax.dot_general(k_aug[:klen], q_aug, _NT, preferred_element_type=F32))
        return out

    def finish(qi, s_both):
        q0 = qi * TQ
        klen = q0 + TQ
        outs = []
        for hh in range(2):
            s_t = s_both[hh]
            s_diag = jnp.where(causal, s_t[q0:], -jnp.inf)
            m = jnp.max(s_diag, axis=0, keepdims=True)
            if qi > 0:
                m = jnp.maximum(m, jnp.max(s_t[:q0], axis=0, keepdims=True))
            o_t, l = None, None
            for k0 in range(0, klen, TQ):
                p_t = jnp.exp2((s_diag if k0 == q0 else s_t[k0:k0 + TQ]) - m)
                l_t = jnp.sum(p_t, axis=0, keepdims=True)
                part = jnp.dot(v_t[hh * FOX_DH:(hh + 1) * FOX_DH, k0:k0 + TQ], p_t.astype(BF16),
                               preferred_element_type=F32)
                o_t = part if o_t is None else o_t + part
                l = l_t if l is None else l + l_t
            outs.append(o_t * (1.0 / l))
        o_ref[0, q0:klen, :] = jnp.concatenate(outs, axis=0).T.astype(o_ref.dtype)

    n_blocks = seq // TQ
    order = list(range(1, n_blocks, 2)) + list(range(n_blocks - 2 + n_blocks % 2, -1, -2))
    ready = {qi: logits(qi) for qi in order[:_LOOKAHEAD]}
    for pos, qi in enumerate(order):
        if pos + _LOOKAHEAD < len(order):
            ahead = order[pos + _LOOKAHEAD]
            ready[ahead] = logits(ahead)
        finish(qi, ready.pop(qi))


def _fox_attn(fq, fk, fv, qa, ka):
    b, s, _ = fq.shape
    qkv = pl.BlockSpec((1, s, LANES), lambda i, h: (i, 0, h))
    aug = pl.BlockSpec((1, s, SMALL_W), lambda i, h: (i, 0, 0))
    return pl.pallas_call(
        _attn_kernel,
        out_shape=jax.ShapeDtypeStruct((b, s, FOX_WIDTH), BF16),
        grid=(b, FOX_HEADS // 2),
        in_specs=[qkv, qkv, qkv, aug, aug],
        out_specs=qkv,
        compiler_params=pltpu.CompilerParams(
            dimension_semantics=("arbitrary", "arbitrary"), vmem_limit_bytes=VMEM_LIMIT),
        name="fox_attn",
    )(fq, fk, fv, qa, ka)


def _out_kernel(x_ref, fox_ref, gla_ref, wo_ref, g_ref, wg_ref, wu_ref, wd_ref, gf_ref, o_ref):
    x = x_ref[...]
    x = x + jnp.dot(fox_ref[...], wo_ref[:FOX_WIDTH, :], preferred_element_type=F32)
    x = x + jnp.dot(gla_ref[...], wo_ref[FOX_WIDTH:, :], preferred_element_type=F32)
    h = _rms(x, g_ref[...]).astype(BF16)
    x = x + 0.5 * _swiglu(h, wg_ref, wu_ref, wd_ref)
    o_ref[...] = _rms(x, gf_ref[...])


def _out_ffn2(x1, fox, gla, wo, g, wg, wu, wd, gf):
    n = x1.shape[0]

    def rows(width):
        return pl.BlockSpec((TM_FFN, width), lambda i: (i, 0))

    return pl.pallas_call(
        _out_kernel,
        out_shape=jax.ShapeDtypeStruct((n, D_MODEL), F32),
        grid=(n // TM_FFN,),
        in_specs=[rows(D_MODEL), rows(FOX_WIDTH), rows(GLA_VW),
                  _const_spec((FOX_WIDTH + GLA_VW, D_MODEL)), _const_spec((1, D_MODEL)),
                  _const_spec((D_MODEL, D_FF)), _const_spec((D_MODEL, D_FF)),
                  _const_spec((D_FF, D_MODEL)), _const_spec((1, D_MODEL))],
        out_specs=rows(D_MODEL),
        compiler_params=pltpu.CompilerParams(
            dimension_semantics=("arbitrary",), vmem_limit_bytes=VMEM_LIMIT),
        name="out_ffn2",
    )(x1, fox, gla, wo, g, wg, wu, wd, gf)


def _pack_w_in(w_in):
    widths = (FOX_WIDTH, FOX_WIDTH, FOX_WIDTH, FOX_HEADS, GLA_KW, GLA_KW, GLA_VW, GLA_RANK, GLA_VW)
    offs = [0]
    for wd in widths:
        offs.append(offs[-1] + wd)
    fq, fk, fv, ff, gq, gk, gv, glow, gout = (w_in[:, offs[i]:offs[i + 1]] for i in range(9))
    pad = jnp.zeros((D_MODEL, SMALL_W - _FF_LANES - GLA_RANK), w_in.dtype)
    packed = jnp.concatenate(
        [fq * (_LOG2E * FOX_DH ** -0.5), fk, fv, gq * (GLA_DK ** -0.5), gk, gv, gout,
         jnp.tile(ff, (1, _FF_LANES // FOX_HEADS)), glow, pad], axis=1)
    return packed.astype(BF16)


def kernel(x, ffn1_norm, ffn1_w_gate, ffn1_w_up, ffn1_w_down, mix_norm, w_in, fox_forget_bias,
           gla_w_gate_up, gla_gate_bias, gla_out_norm, w_out, ffn2_norm, ffn2_w_gate, ffn2_w_up,
           ffn2_w_down, final_norm):
    b, s, d = x.shape
    n = b * s
    assert ffn1_norm.shape[0] == 1, "the final norm is fused into the single layer's last call"
    x = x.reshape(n, d)
    for l in range(1):
        x1 = _ffn1(x, ffn1_norm[l].reshape(1, d), ffn1_w_gate[l].astype(BF16),
                   ffn1_w_up[l].astype(BF16), ffn1_w_down[l].astype(BF16))

        bias = jnp.zeros((1, SMALL_W), F32).at[0, :_FF_LANES].set(
            jnp.tile(fox_forget_bias[l], _FF_LANES // FOX_HEADS))
        w2 = jnp.zeros((SMALL_W, GLA_KW), F32).at[_FF_LANES:_FF_LANES + GLA_RANK].set(
            gla_w_gate_up[l]).astype(BF16)
        fq, fk, fv, ka, qa, gla = _mix_in(
            x1.reshape(b, s, d), mix_norm[l].reshape(1, d), _pack_w_in(w_in[l]), bias, w2,
            gla_gate_bias[l].reshape(1, GLA_KW), gla_out_norm[l].reshape(1, GLA_DV))
        fox = _fox_attn(fq, fk, fv, qa, ka)

        x = _out_ffn2(x1, fox.reshape(n, -1), gla.reshape(n, -1), w_out[l].astype(BF16),
                      ffn2_norm[l].reshape(1, d), ffn2_w_gate[l].astype(BF16),
                      ffn2_w_up[l].astype(BF16), ffn2_w_down[l].astype(BF16),
                      final_norm.reshape(1, d))
    return x.reshape(b, s, d)
```

```python
import jax
import jax.numpy as jnp
from jax import lax
from jax.experimental import pallas as pl
from jax.experimental.pallas import tpu as pltpu

F32 = jnp.float32
BF16 = jnp.bfloat16

D_MODEL = 1024
D_FF = 2816
EPS = 1e-6
CHUNK = 64

FOX_HEADS = 8
FOX_DH = 64
FOX_WIDTH = FOX_HEADS * FOX_DH
GLA_HEADS = 4
GLA_DK = 64
GLA_DV = 128
GLA_KW = GLA_HEADS * GLA_DK
GLA_VW = GLA_HEADS * GLA_DV
GLA_RANK = 16
GLA_TEMP = 16.0

LANES = 128
SMALL_W = LANES
_C_FQ = 0
_C_FK = _C_FQ + FOX_WIDTH
_C_FV = _C_FK + FOX_WIDTH
_C_GQ = _C_FV + FOX_WIDTH
_C_GK = _C_GQ + GLA_KW
_C_GV = _C_GK + GLA_KW
_C_GO = _C_GV + GLA_VW
_C_SM = _C_GO + GLA_VW
PROJ_W = _C_SM + SMALL_W

TM_FFN = 1024
FF_BOUNDS = (0, 1024, 2048, D_FF)
NORM_BLOCKS = 4
TAIL_BLOCKS = 2
TM_PROJ = 1024
TQ = 256
VMEM_LIMIT = 56 * 1024 * 1024


def _rms(x, g):
    return x * lax.rsqrt(jnp.mean(x * x, axis=-1, keepdims=True) + EPS) * g


def _log_sigmoid(z):
    return jnp.minimum(z, 0.0) - jnp.log(1.0 + jnp.exp(-jnp.abs(z)))


def _row_blocks(n_rows, n_blocks):
    step = n_rows // n_blocks
    return [slice(r0, r0 + step) for r0 in range(0, n_rows, step)]


def _rms_blocked(x, g):
    return jnp.concatenate(
        [_rms(x[rows], g).astype(BF16) for rows in _row_blocks(x.shape[0], NORM_BLOCKS)], axis=0)


def _swiglu(h, wg_ref, wu_ref, wd_ref):
    y = None
    chunks = list(zip(FF_BOUNDS[:-1], FF_BOUNDS[1:]))
    for c0, c1 in chunks:
        gate = jnp.dot(h, wg_ref[:, c0:c1], preferred_element_type=F32)
        up = jnp.dot(h, wu_ref[:, c0:c1], preferred_element_type=F32)
        act = (gate * jax.nn.sigmoid(gate) * up).astype(BF16)
        if (c0, c1) != chunks[-1]:
            part = jnp.dot(act, wd_ref[c0:c1, :], preferred_element_type=F32)
            y = part if y is None else y + part
    out = []
    for rows in _row_blocks(h.shape[0], TAIL_BLOCKS):
        part = jnp.dot(act[rows], wd_ref[c0:c1, :], preferred_element_type=F32)
        out.append((rows, part if y is None else y[rows] + part))
    return out


def _split3(x):
    hi = x.astype(BF16)
    r1 = x - hi.astype(F32)
    mid = r1.astype(BF16)
    lo = (r1 - mid.astype(F32)).astype(BF16)
    return hi, mid, lo


def _dot3(sel, x):
    hi, mid, lo = _split3(x)
    return (jnp.dot(sel, hi, preferred_element_type=F32)
            + jnp.dot(sel, mid, preferred_element_type=F32)
            + jnp.dot(sel, lo, preferred_element_type=F32))


def _const_spec(shape):
    n = len(shape)
    return pl.BlockSpec(shape, lambda *_: (0,) * n, pipeline_mode=pl.Buffered(1))


def _ffn1_kernel(x_ref, g_ref, wg_ref, wu_ref, wd_ref, o_ref):
    h = _rms_blocked(x_ref[...], g_ref[...])
    for rows, y in _swiglu(h, wg_ref, wu_ref, wd_ref):
        o_ref[rows, :] = x_ref[rows, :] + 0.5 * y


def _ffn1(x, g, wg, wu, wd):
    n = x.shape[0]
    row = pl.BlockSpec((TM_FFN, D_MODEL), lambda i: (i, 0))
    return pl.pallas_call(
        _ffn1_kernel,
        out_shape=jax.ShapeDtypeStruct((n, D_MODEL), F32),
        grid=(n // TM_FFN,),
        in_specs=[row, _const_spec((1, D_MODEL)), _const_spec((D_MODEL, D_FF)),
                  _const_spec((D_MODEL, D_FF)), _const_spec((D_FF, D_MODEL))],
        out_specs=row,
        compiler_params=pltpu.CompilerParams(
            dimension_semantics=("arbitrary",), vmem_limit_bytes=VMEM_LIMIT),
        name="ffn1",
    )(x, g, wg, wu, wd)


_DECAY_BLK = 256
_N_TERMS = 3
_BIAS_LANES = _N_TERMS * FOX_HEADS
_FF_LANES = 2 * _BIAS_LANES
_LOG2E = 1.4426950408889634
_NT = (((1,), (1,)), ((), ()))
_TN = (((0,), (0,)), ((), ()))
_N_CHUNK = TM_PROJ // CHUNK
_FOX_PIECE = 256
_DK_SHIFT = GLA_DK.bit_length() - 1


def _decay_operands(small, bias, offset):
    r = lax.broadcasted_iota(jnp.int32, (_DECAY_BLK, _DECAY_BLK), 0)
    c = lax.broadcasted_iota(jnp.int32, (_DECAY_BLK, _DECAY_BLK), 1)
    tri = (r >= c).astype(BF16)
    lane = lax.broadcasted_iota(jnp.int32, (_DECAY_BLK, SMALL_W), 1)
    term = (lane >> 3) % _N_TERMS
    blocks = [slice(b0, b0 + _DECAY_BLK) for b0 in range(0, small.shape[0], _DECAY_BLK)]
    local = []
    for rows in blocks:
        log_f = _log_sigmoid(small[rows] + bias)
        local.append(_dot3(tri, jnp.where(lane < _FF_LANES, log_f, 0.0)))
    ka, qa = [], []
    for cum_local in local:
        cum = cum_local + offset
        offset = cum[_DECAY_BLK - 1:_DECAY_BLK, :]
        hi, mid, lo = _split3(cum * _LOG2E)
        f_term = jnp.where(term == 0, hi.astype(F32),
                           jnp.where(term == 1, mid.astype(F32), lo.astype(F32)))
        one = jnp.ones_like(f_term)
        zero = jnp.zeros_like(f_term)
        ka.append(jnp.where(lane < _BIAS_LANES, -f_term,
                            jnp.where(lane < _FF_LANES, one, zero)).astype(BF16))
        qa.append(jnp.where(lane < _BIAS_LANES, one,
                            jnp.where(lane < _FF_LANES, f_term, zero)).astype(BF16))
    return jnp.concatenate(ka, axis=0), jnp.concatenate(qa, axis=0), offset


def _gla_tile(gq, gk, gv, go, small, w2, b2, g_norm, state, o_ref, fillers):
    fillers = list(fillers)
    every = max(1, (2 * _N_CHUNK) // max(1, len(fillers)))

    def maybe_fill(step):
        if fillers and step % every == every - 1:
            fillers.pop(0)()

    z = jnp.dot(small.astype(BF16), w2, preferred_element_type=F32) + b2
    log_a = _log_sigmoid(z) / GLA_TEMP

    r = lax.broadcasted_iota(jnp.int32, (CHUNK, CHUNK), 0)
    c = lax.broadcasted_iota(jnp.int32, (CHUNK, CHUNK), 1)
    tri = (r >= c).astype(BF16)
    terms = jnp.concatenate(_split3(log_a), axis=1)
    chunks = [slice(n * CHUNK, (n + 1) * CHUNK) for n in range(_N_CHUNK)]
    lane_k = lax.broadcasted_iota(jnp.int32, (CHUNK, GLA_KW), 1) >> _DK_SHIFT

    deltas, decays = [], []
    for n, rows in enumerate(chunks):
        maybe_fill(n)
        cum3 = jnp.dot(tri, terms[rows], preferred_element_type=F32)
        g_cum = cum3[:, :GLA_KW] + cum3[:, GLA_KW:2 * GLA_KW] + cum3[:, 2 * GLA_KW:]
        g_tot = g_cum[CHUNK - 1:CHUNK, :]
        k_dec = (gk[rows] * jnp.exp(g_tot - g_cum)).astype(BF16)
        delta = None
        for h in range(GLA_HEADS):
            part = lax.dot_general(gv[rows, h * GLA_DV:(h + 1) * GLA_DV],
                                   jnp.where(lane_k == h, k_dec, jnp.zeros_like(k_dec)), _TN,
                                   preferred_element_type=F32)
            delta = part if delta is None else delta + part
        deltas.append(delta)
        decays.append(jnp.exp(g_tot))

    states = []
    for delta, decay in zip(deltas, decays):
        state = decay * state + delta
        states.append(state.astype(BF16))

    for n, (rows, st) in enumerate(zip(chunks, states)):
        maybe_fill(_N_CHUNK + n)
        qc = gq[rows]
        q_stack = jnp.concatenate(
            [jnp.where(lane_k == h, qc, jnp.zeros_like(qc)) for h in range(GLA_HEADS)], axis=0)
        o_all = lax.dot_general(q_stack, st, _NT, preferred_element_type=F32)
        for h in range(GLA_HEADS):
            o = _rms(o_all[h * CHUNK:(h + 1) * CHUNK, :], g_norm)
            gate = go[rows, h * GLA_DV:(h + 1) * GLA_DV]
            o_ref[0, rows, h * GLA_DV:(h + 1) * GLA_DV] = (
                o * (gate * jax.nn.sigmoid(gate))).astype(o_ref.dtype)
    for fill in fillers:
        fill()
    return state


def _mix_in_kernel(x_ref, g_ref, w_ref, fb_ref, w2_ref, b2_ref, gn_ref,
                   fq_ref, fk_ref, fv_ref, ka_ref, qa_ref, gla_ref, st_ref, off_ref):
    @pl.when(pl.program_id(1) == 0)
    def _():
        st_ref[...] = jnp.zeros_like(st_ref)
        off_ref[...] = jnp.zeros_like(off_ref)

    h = _rms_blocked(x_ref[0], g_ref[...])

    def proj(c0, width):
        return jnp.dot(h, w_ref[:, c0:c0 + width], preferred_element_type=F32)

    small = proj(_C_SM, SMALL_W)
    gk = proj(_C_GK, GLA_KW)
    gv = proj(_C_GV, GLA_VW).astype(BF16)
    gq = proj(_C_GQ, GLA_KW).astype(BF16)
    go = proj(_C_GO, GLA_VW)

    def fox_proj(out_ref, c0, col):
        def run():
            out_ref[0, :, col:col + _FOX_PIECE] = proj(c0 + col, _FOX_PIECE).astype(BF16)
        return run

    fillers = [fox_proj(ref, c0, col)
               for ref, c0 in ((fq_ref, _C_FQ), (fk_ref, _C_FK), (fv_ref, _C_FV))
               for col in range(0, FOX_WIDTH, _FOX_PIECE)]
    st_ref[...] = _gla_tile(gq, gk, gv, go, small, w2_ref[...], b2_ref[...], gn_ref[...],
                            st_ref[...], gla_ref, fillers)
    ka_ref[0], qa_ref[0], off_ref[...] = _decay_operands(small, fb_ref[...], off_ref[...])


def _mix_in(x1, g, w, fox_bias, w2, b2, gn):
    b, s, _ = x1.shape

    def rows(width):
        return pl.BlockSpec((1, TM_PROJ, width), lambda i, t: (i, t, 0))

    widths = (FOX_WIDTH, FOX_WIDTH, FOX_WIDTH, SMALL_W, SMALL_W, GLA_VW)
    return pl.pallas_call(
        _mix_in_kernel,
        out_shape=[jax.ShapeDtypeStruct((b, s, wd), BF16) for wd in widths],
        grid=(b, s // TM_PROJ),
        in_specs=[rows(D_MODEL), _const_spec((1, D_MODEL)), _const_spec((D_MODEL, PROJ_W)),
                  _const_spec((1, SMALL_W)), _const_spec((SMALL_W, GLA_KW)),
                  _const_spec((1, GLA_KW)), _const_spec((1, GLA_DV))],
        out_specs=[rows(wd) for wd in widths],
        scratch_shapes=[pltpu.VMEM((GLA_DV, GLA_KW), F32), pltpu.VMEM((1, SMALL_W), F32)],
        compiler_params=pltpu.CompilerParams(
            dimension_semantics=("arbitrary", "arbitrary"), vmem_limit_bytes=VMEM_LIMIT),
        name="mix_in",
    )(x1, g, w, fox_bias, w2, b2, gn)


_LOOKAHEAD = 2


def _attn_kernel(q_ref, k_ref, v_ref, qa_ref, ka_ref, o_ref):
    hp = pl.program_id(1)
    seq = q_ref.shape[1]
    lane = lax.broadcasted_iota(jnp.int32, (TQ, LANES), 1)
    k_idx = lax.broadcasted_iota(jnp.int32, (TQ, TQ), 0)
    q_idx = lax.broadcasted_iota(jnp.int32, (TQ, TQ), 1)
    causal = k_idx <= q_idx

    k_aug = jnp.concatenate([k_ref[0], ka_ref[0]], axis=1)
    v_t = v_ref[0].astype(F32).T.astype(BF16)

    def logits(qi):
        q0 = qi * TQ
        klen = q0 + TQ
        q = q_ref[0, q0:klen, :]
        qa = qa_ref[0, q0:klen, :]
        out = []
        for hh in range(2):
            in_head = (lane >= hh * FOX_DH) & (lane < (hh + 1) * FOX_DH)
            bias_lane = ((lane & (FOX_HEADS - 1)) == 2 * hp + hh) & (lane < _FF_LANES)
            q_aug = jnp.concatenate([jnp.where(in_head, q, jnp.zeros_like(q)),
                                     jnp.where(bias_lane, qa, jnp.zeros_like(qa))], axis=1)
            out.append(lax.dot_general(k_aug[:klen], q_aug, _NT, preferred_element_type=F32))
        return out

    def finish(qi, s_both):
        q0 = qi * TQ
        klen = q0 + TQ
        outs = []
        for hh in range(2):
            s_t = s_both[hh]
            s_diag = jnp.where(causal, s_t[q0:], -jnp.inf)
            m = jnp.max(s_diag, axis=0, keepdims=True)
            if qi > 0:
                m = jnp.maximum(m, jnp.max(s_t[:q0], axis=0, keepdims=True))
            o_t, l = None, None
            for k0 in range(0, klen, TQ):
                p_t = jnp.exp2((s_diag if k0 == q0 else s_t[k0:k0 + TQ]) - m)
                l_t = jnp.sum(p_t, axis=0, keepdims=True)
                part = jnp.dot(v_t[hh * FOX_DH:(hh + 1) * FOX_DH, k0:k0 + TQ], p_t.astype(BF16),
                               preferred_element_type=F32)
                o_t = part if o_t is None else o_t + part
                l = l_t if l is None else l + l_t
            outs.append(o_t * (1.0 / l))
        o_ref[0, q0:klen, :] = jnp.concatenate(outs, axis=0).T.astype(o_ref.dtype)

    n_blocks = seq // TQ
    order = list(range(1, n_blocks, 2)) + list(range(n_blocks - 2 + n_blocks % 2, -1, -2))
    ready = {qi: logits(qi) for qi in order[:_LOOKAHEAD]}
    for pos, qi in enumerate(order):
        if pos + _LOOKAHEAD < len(order):
            ahead = order[pos + _LOOKAHEAD]
            ready[ahead] = logits(ahead)
        finish(qi, ready.pop(qi))


def _fox_attn(fq, fk, fv, qa, ka):
    b, s, _ = fq.shape
    qkv = pl.BlockSpec((1, s, LANES), lambda i, h: (i, 0, h))
    aug = pl.BlockSpec((1, s, SMALL_W), lambda i, h: (i, 0, 0))
    return pl.pallas_call(
        _attn_kernel,
        out_shape=jax.ShapeDtypeStruct((b, s, FOX_WIDTH), BF16),
        grid=(b, FOX_HEADS // 2),
        in_specs=[qkv, qkv, qkv, aug, aug],
        out_specs=qkv,
        compiler_params=pltpu.CompilerParams(
            dimension_semantics=("arbitrary", "arbitrary"), vmem_limit_bytes=VMEM_LIMIT),
        name="fox_attn",
    )(fq, fk, fv, qa, ka)


def _out_kernel(x_ref, fox_ref, gla_ref, wo_ref, g_ref, wg_ref, wu_ref, wd_ref, gf_ref, o_ref):
    x2, h = [], []
    for rows in _row_blocks(x_ref.shape[0], NORM_BLOCKS):
        mix = (jnp.dot(fox_ref[rows, :], wo_ref[:FOX_WIDTH, :], preferred_element_type=F32)
               + jnp.dot(gla_ref[rows, :], wo_ref[FOX_WIDTH:, :], preferred_element_type=F32))
        x2.append(x_ref[rows, :] + mix)
        h.append(_rms(x2[-1], g_ref[...]).astype(BF16))
    x2 = jnp.concatenate(x2, axis=0)
    for rows, y in _swiglu(jnp.concatenate(h, axis=0), wg_ref, wu_ref, wd_ref):
        o_ref[rows, :] = _rms(x2[rows] + 0.5 * y, gf_ref[...])


def _out_ffn2(x1, fox, gla, wo, g, wg, wu, wd, gf):
    n = x1.shape[0]

    def rows(width):
        return pl.BlockSpec((TM_FFN, width), lambda i: (i, 0))

    return pl.pallas_call(
        _out_kernel,
        out_shape=jax.ShapeDtypeStruct((n, D_MODEL), F32),
        grid=(n // TM_FFN,),
        in_specs=[rows(D_MODEL), rows(FOX_WIDTH), rows(GLA_VW),
                  _const_spec((FOX_WIDTH + GLA_VW, D_MODEL)), _const_spec((1, D_MODEL)),
                  _const_spec((D_MODEL, D_FF)), _const_spec((D_MODEL, D_FF)),
                  _const_spec((D_FF, D_MODEL)), _const_spec((1, D_MODEL))],
        out_specs=rows(D_MODEL),
        compiler_params=pltpu.CompilerParams(
            dimension_semantics=("arbitrary",), vmem_limit_bytes=VMEM_LIMIT),
        name="out_ffn2",
    )(x1, fox, gla, wo, g, wg, wu, wd, gf)


def _pack_w_in(w_in):
    widths = (FOX_WIDTH, FOX_WIDTH, FOX_WIDTH, FOX_HEADS, GLA_KW, GLA_KW, GLA_VW, GLA_RANK, GLA_VW)
    offs = [0]
    for wd in widths:
        offs.append(offs[-1] + wd)
    w_t = w_in.T
    fq, fk, fv, ff, gq, gk, gv, glow, gout = (w_t[offs[i]:offs[i + 1]] for i in range(9))
    pad = jnp.zeros((SMALL_W - _FF_LANES - GLA_RANK, D_MODEL), w_in.dtype)
    packed = jnp.concatenate(
        [fq * (_LOG2E * FOX_DH ** -0.5), fk, fv, gq * (GLA_DK ** -0.5), gk, gv, gout,
         jnp.tile(ff, (_FF_LANES // FOX_HEADS, 1)), glow, pad], axis=0)
    return packed.astype(BF16).T


def kernel(x, ffn1_norm, ffn1_w_gate, ffn1_w_up, ffn1_w_down, mix_norm, w_in, fox_forget_bias,
           gla_w_gate_up, gla_gate_bias, gla_out_norm, w_out, ffn2_norm, ffn2_w_gate, ffn2_w_up,
           ffn2_w_down, final_norm):
    b, s, d = x.shape
    n = b * s
    assert ffn1_norm.shape[0] == 1, "the final norm is fused into the single layer's last call"
    x = x.reshape(n, d)
    for l in range(1):
        x1 = _ffn1(x, ffn1_norm[l].reshape(1, d), ffn1_w_gate[l].astype(BF16),
                   ffn1_w_up[l].astype(BF16), ffn1_w_down[l].astype(BF16))

        bias = jnp.zeros((1, SMALL_W), F32).at[0, :_FF_LANES].set(
            jnp.tile(fox_forget_bias[l], _FF_LANES // FOX_HEADS))
        w2 = jnp.zeros((SMALL_W, GLA_KW), F32).at[_FF_LANES:_FF_LANES + GLA_RANK].set(
            gla_w_gate_up[l]).astype(BF16)
        fq, fk, fv, ka, qa, gla = _mix_in(
            x1.reshape(b, s, d), mix_norm[l].reshape(1, d), _pack_w_in(w_in[l]), bias, w2,
            gla_gate_bias[l].reshape(1, GLA_KW), gla_out_norm[l].reshape(1, GLA_DV))
        fox = _fox_attn(fq, fk, fv, qa, ka)

        x = _out_ffn2(x1, fox.reshape(n, -1), gla.reshape(n, -1), w_out[l].astype(BF16),
                      ffn2_norm[l].reshape(1, d), ffn2_w_gate[l].astype(BF16),
                      ffn2_w_up[l].astype(BF16), ffn2_w_down[l].astype(BF16),
                      final_norm.reshape(1, d))
    return x.reshape(b, s, d)
```

```python
import jax
import jax.numpy as jnp
from jax import lax
from jax.experimental import pallas as pl
from jax.experimental.pallas import tpu as pltpu

F32 = jnp.float32
BF16 = jnp.bfloat16

D_MODEL = 1024
D_FF = 2816
EPS = 1e-6
CHUNK = 64

FOX_HEADS = 8
FOX_DH = 64
FOX_WIDTH = FOX_HEADS * FOX_DH
GLA_HEADS = 4
GLA_DK = 64
GLA_DV = 128
GLA_KW = GLA_HEADS * GLA_DK
GLA_VW = GLA_HEADS * GLA_DV
GLA_RANK = 16
GLA_TEMP = 16.0

LANES = 128
SMALL_W = LANES
_C_FQ = 0
_C_FK = _C_FQ + FOX_WIDTH
_C_FV = _C_FK + FOX_WIDTH
_C_GQ = _C_FV + FOX_WIDTH
_C_GK = _C_GQ + GLA_KW
_C_GV = _C_GK + GLA_KW
_C_GO = _C_GV + GLA_VW
_C_SM = _C_GO + GLA_VW
PROJ_W = _C_SM + SMALL_W

TM_FFN = 1024
FF_BOUNDS = (0, 1024, 2048, D_FF)
NORM_BLOCKS = 4
TAIL_BLOCKS = 2
TM_PROJ = 1024
TQ = 256
VMEM_LIMIT = 56 * 1024 * 1024


def _rms(x, g):
    return x * lax.rsqrt(jnp.mean(x * x, axis=-1, keepdims=True) + EPS) * g


def _log_sigmoid(z):
    return jnp.minimum(z, 0.0) - jnp.log(1.0 + jnp.exp(-jnp.abs(z)))


def _row_blocks(n_rows, n_blocks):
    step = n_rows // n_blocks
    return [slice(r0, r0 + step) for r0 in range(0, n_rows, step)]


def _rms_blocked(x, g):
    return jnp.concatenate(
        [_rms(x[rows], g).astype(BF16) for rows in _row_blocks(x.shape[0], NORM_BLOCKS)], axis=0)


def _swiglu(h, wg_ref, wu_ref, wd_ref):
    y = None
    chunks = list(zip(FF_BOUNDS[:-1], FF_BOUNDS[1:]))
    for c0, c1 in chunks:
        gate = jnp.dot(h, wg_ref[:, c0:c1], preferred_element_type=F32)
        up = jnp.dot(h, wu_ref[:, c0:c1], preferred_element_type=F32)
        act = (gate * jax.nn.sigmoid(gate) * up).astype(BF16)
        if (c0, c1) != chunks[-1]:
            part = jnp.dot(act, wd_ref[c0:c1, :], preferred_element_type=F32)
            y = part if y is None else y + part
    out = []
    for rows in _row_blocks(h.shape[0], TAIL_BLOCKS):
        part = jnp.dot(act[rows], wd_ref[c0:c1, :], preferred_element_type=F32)
        out.append((rows, part if y is None else y[rows] + part))
    return out


def _split3(x):
    hi = x.astype(BF16)
    r1 = x - hi.astype(F32)
    mid = r1.astype(BF16)
    lo = (r1 - mid.astype(F32)).astype(BF16)
    return hi, mid, lo


def _dot3(sel, x):
    hi, mid, lo = _split3(x)
    return (jnp.dot(sel, hi, preferred_element_type=F32)
            + jnp.dot(sel, mid, preferred_element_type=F32)
            + jnp.dot(sel, lo, preferred_element_type=F32))


def _const_spec(shape):
    n = len(shape)
    return pl.BlockSpec(shape, lambda *_: (0,) * n, pipeline_mode=pl.Buffered(1))


def _ffn1_kernel(x_ref, g_ref, wg_ref, wu_ref, wd_ref, o_ref):
    h = _rms_blocked(x_ref[...], g_ref[...])
    for rows, y in _swiglu(h, wg_ref, wu_ref, wd_ref):
        o_ref[rows, :] = x_ref[rows, :] + 0.5 * y


def _ffn1(x, g, wg, wu, wd):
    n = x.shape[0]
    row = pl.BlockSpec((TM_FFN, D_MODEL), lambda i: (i, 0))
    return pl.pallas_call(
        _ffn1_kernel,
        out_shape=jax.ShapeDtypeStruct((n, D_MODEL), F32),
        grid=(n // TM_FFN,),
        in_specs=[row, _const_spec((1, D_MODEL)), _const_spec((D_MODEL, D_FF)),
                  _const_spec((D_MODEL, D_FF)), _const_spec((D_FF, D_MODEL))],
        out_specs=row,
        compiler_params=pltpu.CompilerParams(
            dimension_semantics=("arbitrary",), vmem_limit_bytes=VMEM_LIMIT),
        name="ffn1",
    )(x, g, wg, wu, wd)


_DECAY_BLK = 256
_N_TERMS = 3
_BIAS_LANES = _N_TERMS * FOX_HEADS
_FF_LANES = 2 * _BIAS_LANES
_LOG2E = 1.4426950408889634
_NT = (((1,), (1,)), ((), ()))
_TN = (((0,), (0,)), ((), ()))
_N_CHUNK = TM_PROJ // CHUNK
_FOX_PIECE = 256
_DK_SHIFT = GLA_DK.bit_length() - 1


def _decay_operands(small, bias, offset):
    r = lax.broadcasted_iota(jnp.int32, (_DECAY_BLK, _DECAY_BLK), 0)
    c = lax.broadcasted_iota(jnp.int32, (_DECAY_BLK, _DECAY_BLK), 1)
    tri = (r >= c).astype(BF16)
    lane = lax.broadcasted_iota(jnp.int32, (_DECAY_BLK, SMALL_W), 1)
    term = (lane >> 3) % _N_TERMS
    blocks = [slice(b0, b0 + _DECAY_BLK) for b0 in range(0, small.shape[0], _DECAY_BLK)]
    local = []
    for rows in blocks:
        log_f = _log_sigmoid(small[rows] + bias)
        local.append(_dot3(tri, jnp.where(lane < _FF_LANES, log_f, 0.0)))
    ka, qa = [], []
    for cum_local in local:
        cum = cum_local + offset
        offset = cum[_DECAY_BLK - 1:_DECAY_BLK, :]
        hi, mid, lo = _split3(cum * _LOG2E)
        f_term = jnp.where(term == 0, hi.astype(F32),
                           jnp.where(term == 1, mid.astype(F32), lo.astype(F32)))
        one = jnp.ones_like(f_term)
        zero = jnp.zeros_like(f_term)
        ka.append(jnp.where(lane < _BIAS_LANES, -f_term,
                            jnp.where(lane < _FF_LANES, one, zero)).astype(BF16))
        qa.append(jnp.where(lane < _BIAS_LANES, one,
                            jnp.where(lane < _FF_LANES, f_term, zero)).astype(BF16))
    return jnp.concatenate(ka, axis=0), jnp.concatenate(qa, axis=0), offset


def _gla_tile(gq, gk, gv, go, small, w2, b2, g_norm, state, o_ref, fillers):
    fillers = list(fillers)
    every = max(1, (2 * _N_CHUNK) // max(1, len(fillers)))

    def maybe_fill(step):
        if fillers and step % every == every - 1:
            fillers.pop(0)()

    z = jnp.dot(small.astype(BF16), w2, preferred_element_type=F32) + b2
    log_a = _log_sigmoid(z) / GLA_TEMP

    r = lax.broadcasted_iota(jnp.int32, (CHUNK, CHUNK), 0)
    c = lax.broadcasted_iota(jnp.int32, (CHUNK, CHUNK), 1)
    tri = (r >= c).astype(BF16)
    terms = jnp.concatenate(_split3(log_a), axis=1)
    chunks = [slice(n * CHUNK, (n + 1) * CHUNK) for n in range(_N_CHUNK)]
    lane_k = lax.broadcasted_iota(jnp.int32, (CHUNK, GLA_KW), 1) >> _DK_SHIFT

    deltas, decays = [], []
    for n, rows in enumerate(chunks):
        maybe_fill(n)
        cum3 = jnp.dot(tri, terms[rows], preferred_element_type=F32)
        g_cum = cum3[:, :GLA_KW] + cum3[:, GLA_KW:2 * GLA_KW] + cum3[:, 2 * GLA_KW:]
        g_tot = g_cum[CHUNK - 1:CHUNK, :]
        k_dec = (gk[rows] * jnp.exp(g_tot - g_cum)).astype(BF16)
        delta = None
        for h in range(GLA_HEADS):
            part = lax.dot_general(gv[rows, h * GLA_DV:(h + 1) * GLA_DV],
                                   jnp.where(lane_k == h, k_dec, jnp.zeros_like(k_dec)), _TN,
                                   preferred_element_type=F32)
            delta = part if delta is None else delta + part
        deltas.append(delta)
        decays.append(jnp.exp(g_tot))

    states = []
    for delta, decay in zip(deltas, decays):
        state = decay * state + delta
        states.append(state.astype(BF16))

    for n, (rows, st) in enumerate(zip(chunks, states)):
        maybe_fill(_N_CHUNK + n)
        qc = gq[rows]
        q_stack = jnp.concatenate(
            [jnp.where(lane_k == h, qc, jnp.zeros_like(qc)) for h in range(GLA_HEADS)], axis=0)
        o_all = lax.dot_general(q_stack, st, _NT, preferred_element_type=F32)
        for h in range(GLA_HEADS):
            o = _rms(o_all[h * CHUNK:(h + 1) * CHUNK, :], g_norm)
            gate = go[rows, h * GLA_DV:(h + 1) * GLA_DV]
            o_ref[0, rows, h * GLA_DV:(h + 1) * GLA_DV] = (
                o * (gate * jax.nn.sigmoid(gate))).astype(o_ref.dtype)
    for fill in fillers:
        fill()
    return state


def _mix_in_kernel(x_ref, g_ref, w_ref, fb_ref, w2_ref, b2_ref, gn_ref,
                   fq_ref, fk_ref, fv_ref, ka_ref, qa_ref, gla_ref, st_ref, off_ref):
    @pl.when(pl.program_id(1) == 0)
    def _():
        st_ref[...] = jnp.zeros_like(st_ref)
        off_ref[...] = jnp.zeros_like(off_ref)

    h = _rms_blocked(x_ref[0], g_ref[...])

    def proj(c0, width):
        return jnp.dot(h, w_ref[:, c0:c0 + width], preferred_element_type=F32)

    small = proj(_C_SM, SMALL_W)
    gk = proj(_C_GK, GLA_KW)
    gv = proj(_C_GV, GLA_VW).astype(BF16)
    gq = proj(_C_GQ, GLA_KW).astype(BF16)
    go = proj(_C_GO, GLA_VW)

    def fox_proj(out_ref, c0, col):
        def run():
            out_ref[0, :, col:col + _FOX_PIECE] = proj(c0 + col, _FOX_PIECE).astype(BF16)
        return run

    fillers = [fox_proj(ref, c0, col)
               for ref, c0 in ((fq_ref, _C_FQ), (fk_ref, _C_FK), (fv_ref, _C_FV))
               for col in range(0, FOX_WIDTH, _FOX_PIECE)]
    st_ref[...] = _gla_tile(gq, gk, gv, go, small, w2_ref[...], b2_ref[...], gn_ref[...],
                            st_ref[...], gla_ref, fillers)
    ka_ref[0], qa_ref[0], off_ref[...] = _decay_operands(small, fb_ref[...], off_ref[...])


def _mix_in(x1, g, w, fox_bias, w2, b2, gn):
    b, s, _ = x1.shape

    def rows(width):
        return pl.BlockSpec((1, TM_PROJ, width), lambda i, t: (i, t, 0))

    widths = (FOX_WIDTH, FOX_WIDTH, FOX_WIDTH, SMALL_W, SMALL_W, GLA_VW)
    return pl.pallas_call(
        _mix_in_kernel,
        out_shape=[jax.ShapeDtypeStruct((b, s, wd), BF16) for wd in widths],
        grid=(b, s // TM_PROJ),
        in_specs=[rows(D_MODEL), _const_spec((1, D_MODEL)), _const_spec((D_MODEL, PROJ_W)),
                  _const_spec((1, SMALL_W)), _const_spec((SMALL_W, GLA_KW)),
                  _const_spec((1, GLA_KW)), _const_spec((1, GLA_DV))],
        out_specs=[rows(wd) for wd in widths],
        scratch_shapes=[pltpu.VMEM((GLA_DV, GLA_KW), F32), pltpu.VMEM((1, SMALL_W), F32)],
        compiler_params=pltpu.CompilerParams(
            dimension_semantics=("arbitrary", "arbitrary"), vmem_limit_bytes=VMEM_LIMIT),
        name="mix_in",
    )(x1, g, w, fox_bias, w2, b2, gn)


_LOOKAHEAD = 2
PAIRS_PER_STEP = 2

def _attn_kernel(q_ref, k_ref, v_ref, qa_ref, ka_ref, o_ref):
    seq = q_ref.shape[1]
    lane = lax.broadcasted_iota(jnp.int32, (TQ, LANES), 1)
    k_idx = lax.broadcasted_iota(jnp.int32, (TQ, TQ), 0)
    q_idx = lax.broadcasted_iota(jnp.int32, (TQ, TQ), 1)
    causal = k_idx <= q_idx
    pair_lanes = [slice(p * LANES, (p + 1) * LANES) for p in range(PAIRS_PER_STEP)]

    k_aug = [jnp.concatenate([k_ref[0, :, cols], ka_ref[0]], axis=1) for cols in pair_lanes]
    v_t = [v_ref[0, :, cols].astype(F32).T.astype(BF16) for cols in pair_lanes]

    def logits(p, qi):
        q0 = qi * TQ
        klen = q0 + TQ
        q = q_ref[0, q0:klen, pair_lanes[p]]
        qa = qa_ref[0, q0:klen, :]
        head0 = 2 * (pl.program_id(1) * PAIRS_PER_STEP + p)
        out = []
        for hh in range(2):
            in_head = (lane >= hh * FOX_DH) & (lane < (hh + 1) * FOX_DH)
            bias_lane = ((lane & (FOX_HEADS - 1)) == head0 + hh) & (lane < _FF_LANES)
            q_aug = jnp.concatenate([jnp.where(in_head, q, jnp.zeros_like(q)),
                                     jnp.where(bias_lane, qa, jnp.zeros_like(qa))], axis=1)
            out.append(lax.dot_general(k_aug[p][:klen], q_aug, _NT, preferred_element_type=F32))
        return out

    def finish(p, qi, s_both):
        q0 = qi * TQ
        klen = q0 + TQ
        outs = []
        for hh in range(2):
            s_t = s_both[hh]
            s_diag = jnp.where(causal, s_t[q0:], -jnp.inf)
            m = jnp.max(s_diag, axis=0, keepdims=True)
            if qi > 0:
                m = jnp.maximum(m, jnp.max(s_t[:q0], axis=0, keepdims=True))
            o_t, l = None, None
            for k0 in range(0, klen, TQ):
                p_t = jnp.exp2((s_diag if k0 == q0 else s_t[k0:k0 + TQ]) - m)
                l_t = jnp.sum(p_t, axis=0, keepdims=True)
                part = jnp.dot(v_t[p][hh * FOX_DH:(hh + 1) * FOX_DH, k0:k0 + TQ],
                               p_t.astype(BF16), preferred_element_type=F32)
                o_t = part if o_t is None else o_t + part
                l = l_t if l is None else l + l_t
            outs.append(o_t * (1.0 / l))
        o_ref[0, q0:klen, pair_lanes[p]] = jnp.concatenate(outs, axis=0).T.astype(o_ref.dtype)

    n_blocks = seq // TQ
    order = list(range(1, n_blocks, 2)) + list(range(n_blocks - 2 + n_blocks % 2, -1, -2))
    items = [(p, qi) for p in range(PAIRS_PER_STEP) for qi in order]
    ready = {item: logits(*item) for item in items[:_LOOKAHEAD]}
    for pos, item in enumerate(items):
        if pos + _LOOKAHEAD < len(items):
            ahead = items[pos + _LOOKAHEAD]
            ready[ahead] = logits(*ahead)
        finish(*item, ready.pop(item))


def _fox_attn(fq, fk, fv, qa, ka):
    b, s, _ = fq.shape
    qkv = pl.BlockSpec((1, s, PAIRS_PER_STEP * LANES), lambda i, h: (i, 0, h))
    aug = pl.BlockSpec((1, s, SMALL_W), lambda i, h: (i, 0, 0))
    return pl.pallas_call(
        _attn_kernel,
        out_shape=jax.ShapeDtypeStruct((b, s, FOX_WIDTH), BF16),
        grid=(b, FOX_HEADS // (2 * PAIRS_PER_STEP)),
        in_specs=[qkv, qkv, qkv, aug, aug],
        out_specs=qkv,
        compiler_params=pltpu.CompilerParams(
            dimension_semantics=("arbitrary", "arbitrary"), vmem_limit_bytes=VMEM_LIMIT),
        name="fox_attn",
    )(fq, fk, fv, qa, ka)


def _out_kernel(x_ref, fox_ref, gla_ref, wo_ref, g_ref, wg_ref, wu_ref, wd_ref, gf_ref, o_ref):
    x2, h = [], []
    for rows in _row_blocks(x_ref.shape[0], NORM_BLOCKS):
        mix = (jnp.dot(fox_ref[rows, :], wo_ref[:FOX_WIDTH, :], preferred_element_type=F32)
               + jnp.dot(gla_ref[rows, :], wo_ref[FOX_WIDTH:, :], preferred_element_type=F32))
        x2.append(x_ref[rows, :] + mix)
        h.append(_rms(x2[-1], g_ref[...]).astype(BF16))
    x2 = jnp.concatenate(x2, axis=0)
    for rows, y in _swiglu(jnp.concatenate(h, axis=0), wg_ref, wu_ref, wd_ref):
        o_ref[rows, :] = _rms(x2[rows] + 0.5 * y, gf_ref[...])


def _out_ffn2(x1, fox, gla, wo, g, wg, wu, wd, gf):
    n = x1.shape[0]

    def rows(width):
        return pl.BlockSpec((TM_FFN, width), lambda i: (i, 0))

    return pl.pallas_call(
        _out_kernel,
        out_shape=jax.ShapeDtypeStruct((n, D_MODEL), F32),
        grid=(n // TM_FFN,),
        in_specs=[rows(D_MODEL), rows(FOX_WIDTH), rows(GLA_VW),
                  _const_spec((FOX_WIDTH + GLA_VW, D_MODEL)), _const_spec((1, D_MODEL)),
                  _const_spec((D_MODEL, D_FF)), _const_spec((D_MODEL, D_FF)),
                  _const_spec((D_FF, D_MODEL)), _const_spec((1, D_MODEL))],
        out_specs=rows(D_MODEL),
        compiler_params=pltpu.CompilerParams(
            dimension_semantics=("arbitrary",), vmem_limit_bytes=VMEM_LIMIT),
        name="out_ffn2",
    )(x1, fox, gla, wo, g, wg, wu, wd, gf)


def _pack_w_in(w_in):
    widths = (FOX_WIDTH, FOX_WIDTH, FOX_WIDTH, FOX_HEADS, GLA_KW, GLA_KW, GLA_VW, GLA_RANK, GLA_VW)
    offs = [0]
    for wd in widths:
        offs.append(offs[-1] + wd)
    w_t = w_in.T
    fq, fk, fv, ff, gq, gk, gv, glow, gout = (w_t[offs[i]:offs[i + 1]] for i in range(9))
    pad = jnp.zeros((SMALL_W - _FF_LANES - GLA_RANK, D_MODEL), w_in.dtype)
    packed = jnp.concatenate(
        [fq * (_LOG2E * FOX_DH ** -0.5), fk, fv, gq * (GLA_DK ** -0.5), gk, gv, gout,
         jnp.tile(ff, (_FF_LANES // FOX_HEADS, 1)), glow, pad], axis=0)
    return packed.astype(BF16).T


def kernel(x, ffn1_norm, ffn1_w_gate, ffn1_w_up, ffn1_w_down, mix_norm, w_in, fox_forget_bias,
           gla_w_gate_up, gla_gate_bias, gla_out_norm, w_out, ffn2_norm, ffn2_w_gate, ffn2_w_up,
           ffn2_w_down, final_norm):
    b, s, d = x.shape
    n = b * s
    assert ffn1_norm.shape[0] == 1, "the final norm is fused into the single layer's last call"
    x = x.reshape(n, d)
    for l in range(1):
        x1 = _ffn1(x, ffn1_norm[l].reshape(1, d), ffn1_w_gate[l].astype(BF16),
                   ffn1_w_up[l].astype(BF16), ffn1_w_down[l].astype(BF16))

        bias = jnp.zeros((1, SMALL_W), F32).at[0, :_FF_LANES].set(
            jnp.tile(fox_forget_bias[l], _FF_LANES // FOX_HEADS))
        w2 = jnp.zeros((SMALL_W, GLA_KW), F32).at[_FF_LANES:_FF_LANES + GLA_RANK].set(
            gla_w_gate_up[l]).astype(BF16)
        fq, fk, fv, ka, qa, gla = _mix_in(
            x1.reshape(b, s, d), mix_norm[l].reshape(1, d), _pack_w_in(w_in[l]), bias, w2,
            gla_gate_bias[l].reshape(1, GLA_KW), gla_out_norm[l].reshape(1, GLA_DV))
        fox = _fox_attn(fq, fk, fv, qa, ka)

        x = _out_ffn2(x1, fox.reshape(n, -1), gla.reshape(n, -1), w_out[l].astype(BF16),
                      ffn2_norm[l].reshape(1, d), ffn2_w_gate[l].astype(BF16),
                      ffn2_w_up[l].astype(BF16), ffn2_w_down[l].astype(BF16),
                      final_norm.reshape(1, d))
    return x.reshape(b, s, d)
```

```python
import jax
import jax.numpy as jnp
from jax import lax
from jax.experimental import pallas as pl
from jax.experimental.pallas import tpu as pltpu

F32 = jnp.float32
BF16 = jnp.bfloat16

D_MODEL = 1024
D_FF = 2816
EPS = 1e-6
CHUNK = 64

FOX_HEADS = 8
FOX_DH = 64
FOX_WIDTH = FOX_HEADS * FOX_DH
GLA_HEADS = 4
GLA_DK = 64
GLA_DV = 128
GLA_KW = GLA_HEADS * GLA_DK
GLA_VW = GLA_HEADS * GLA_DV
GLA_RANK = 16
GLA_TEMP = 16.0

LANES = 128
SMALL_W = LANES
_C_FQ = 0
_C_FK = _C_FQ + FOX_WIDTH
_C_FV = _C_FK + FOX_WIDTH
_C_GQ = _C_FV + FOX_WIDTH
_C_GK = _C_GQ + GLA_KW
_C_GV = _C_GK + GLA_KW
_C_GO = _C_GV + GLA_VW
_C_SM = _C_GO + GLA_VW
PROJ_W = _C_SM + SMALL_W

TM_FFN = 1024
FF_BOUNDS = (0, 1024, 2048, D_FF)
NORM_BLOCKS = 4
TAIL_BLOCKS = 2
TM_PROJ = 1024
TQ = 256
VMEM_LIMIT = 56 * 1024 * 1024


def _rms(x, g):
    return x * lax.rsqrt(jnp.mean(x * x, axis=-1, keepdims=True) + EPS) * g


def _log_sigmoid(z):
    return jnp.minimum(z, 0.0) - jnp.log(1.0 + jnp.exp(-jnp.abs(z)))


def _row_blocks(n_rows, n_blocks):
    step = n_rows // n_blocks
    return [slice(r0, r0 + step) for r0 in range(0, n_rows, step)]


def _rms_blocked(x, g):
    return jnp.concatenate(
        [_rms(x[rows], g).astype(BF16) for rows in _row_blocks(x.shape[0], NORM_BLOCKS)], axis=0)


def _swiglu(h, wg_ref, wu_ref, wd_ref):
    y = None
    chunks = list(zip(FF_BOUNDS[:-1], FF_BOUNDS[1:]))
    for c0, c1 in chunks:
        gate = jnp.dot(h, wg_ref[:, c0:c1], preferred_element_type=F32)
        up = jnp.dot(h, wu_ref[:, c0:c1], preferred_element_type=F32)
        act = (gate * jax.nn.sigmoid(gate) * up).astype(BF16)
        if (c0, c1) != chunks[-1]:
            part = jnp.dot(act, wd_ref[c0:c1, :], preferred_element_type=F32)
            y = part if y is None else y + part
    out = []
    for rows in _row_blocks(h.shape[0], TAIL_BLOCKS):
        part = jnp.dot(act[rows], wd_ref[c0:c1, :], preferred_element_type=F32)
        out.append((rows, part if y is None else y[rows] + part))
    return out


def _split3(x):
    hi = x.astype(BF16)
    r1 = x - hi.astype(F32)
    mid = r1.astype(BF16)
    lo = (r1 - mid.astype(F32)).astype(BF16)
    return hi, mid, lo


def _dot3(sel, x):
    hi, mid, lo = _split3(x)
    return (jnp.dot(sel, hi, preferred_element_type=F32)
            + jnp.dot(sel, mid, preferred_element_type=F32)
            + jnp.dot(sel, lo, preferred_element_type=F32))


def _const_spec(shape):
    n = len(shape)
    return pl.BlockSpec(shape, lambda *_: (0,) * n, pipeline_mode=pl.Buffered(1))


def _ffn1_kernel(x_ref, g_ref, wg_ref, wu_ref, wd_ref, o_ref):
    h = _rms_blocked(x_ref[...], g_ref[...])
    for rows, y in _swiglu(h, wg_ref, wu_ref, wd_ref):
        o_ref[rows, :] = x_ref[rows, :] + 0.5 * y


def _ffn1(x, g, wg, wu, wd):
    n = x.shape[0]
    row = pl.BlockSpec((TM_FFN, D_MODEL), lambda i: (i, 0))
    return pl.pallas_call(
        _ffn1_kernel,
        out_shape=jax.ShapeDtypeStruct((n, D_MODEL), F32),
        grid=(n // TM_FFN,),
        in_specs=[row, _const_spec((1, D_MODEL)), _const_spec((D_MODEL, D_FF)),
                  _const_spec((D_MODEL, D_FF)), _const_spec((D_FF, D_MODEL))],
        out_specs=row,
        compiler_params=pltpu.CompilerParams(
            dimension_semantics=("arbitrary",), vmem_limit_bytes=VMEM_LIMIT),
        name="ffn1",
    )(x, g, wg, wu, wd)


_DECAY_BLK = 256
_N_TERMS = 3
_BIAS_LANES = _N_TERMS * FOX_HEADS
_FF_LANES = 2 * _BIAS_LANES
_LOG2E = 1.4426950408889634
_NT = (((1,), (1,)), ((), ()))
_TN = (((0,), (0,)), ((), ()))
_N_CHUNK = TM_PROJ // CHUNK
_FOX_PIECE = 256
_DK_SHIFT = GLA_DK.bit_length() - 1


def _decay_operands(small, bias, offset):
    r = lax.broadcasted_iota(jnp.int32, (_DECAY_BLK, _DECAY_BLK), 0)
    c = lax.broadcasted_iota(jnp.int32, (_DECAY_BLK, _DECAY_BLK), 1)
    tri = (r >= c).astype(BF16)
    lane = lax.broadcasted_iota(jnp.int32, (_DECAY_BLK, SMALL_W), 1)
    term = (lane >> 3) % _N_TERMS
    blocks = [slice(b0, b0 + _DECAY_BLK) for b0 in range(0, small.shape[0], _DECAY_BLK)]
    local = []
    for rows in blocks:
        log_f = _log_sigmoid(small[rows] + bias)
        local.append(_dot3(tri, jnp.where(lane < _FF_LANES, log_f, 0.0)))
    ka, qa = [], []
    for cum_local in local:
        cum = cum_local + offset
        offset = cum[_DECAY_BLK - 1:_DECAY_BLK, :]
        hi, mid, lo = _split3(cum * _LOG2E)
        f_term = jnp.where(term == 0, hi.astype(F32),
                           jnp.where(term == 1, mid.astype(F32), lo.astype(F32)))
        one = jnp.ones_like(f_term)
        zero = jnp.zeros_like(f_term)
        ka.append(jnp.where(lane < _BIAS_LANES, -f_term,
                            jnp.where(lane < _FF_LANES, one, zero)).astype(BF16))
        qa.append(jnp.where(lane < _BIAS_LANES, one,
                            jnp.where(lane < _FF_LANES, f_term, zero)).astype(BF16))
    return jnp.concatenate(ka, axis=0), jnp.concatenate(qa, axis=0), offset


def _gla_tile(gq, gk, gv, go, small, w2, b2, g_norm, state, o_ref, fillers):
    fillers = list(fillers)
    every = max(1, (2 * _N_CHUNK) // max(1, len(fillers)))

    def maybe_fill(step):
        if fillers and step % every == every - 1:
            fillers.pop(0)()

    z = jnp.dot(small.astype(BF16), w2, preferred_element_type=F32) + b2
    log_a = _log_sigmoid(z) / GLA_TEMP

    r = lax.broadcasted_iota(jnp.int32, (CHUNK, CHUNK), 0)
    c = lax.broadcasted_iota(jnp.int32, (CHUNK, CHUNK), 1)
    tri = (r >= c).astype(BF16)
    terms = jnp.concatenate(_split3(log_a), axis=1)
    chunks = [slice(n * CHUNK, (n + 1) * CHUNK) for n in range(_N_CHUNK)]
    lane_k = lax.broadcasted_iota(jnp.int32, (CHUNK, GLA_KW), 1) >> _DK_SHIFT

    deltas, decays = [], []
    for n, rows in enumerate(chunks):
        maybe_fill(n)
        cum3 = jnp.dot(tri, terms[rows], preferred_element_type=F32)
        g_cum = cum3[:, :GLA_KW] + cum3[:, GLA_KW:2 * GLA_KW] + cum3[:, 2 * GLA_KW:]
        g_tot = g_cum[CHUNK - 1:CHUNK, :]
        k_dec = (gk[rows] * jnp.exp(g_tot - g_cum)).astype(BF16)
        delta = None
        for h in range(GLA_HEADS):
            part = lax.dot_general(gv[rows, h * GLA_DV:(h + 1) * GLA_DV],
                                   jnp.where(lane_k == h, k_dec, jnp.zeros_like(k_dec)), _TN,
                                   preferred_element_type=F32)
            delta = part if delta is None else delta + part
        deltas.append(delta)
        decays.append(jnp.exp(g_tot))

    states = []
    for delta, decay in zip(deltas, decays):
        state = decay * state + delta
        states.append(state.astype(BF16))

    for n, (rows, st) in enumerate(zip(chunks, states)):
        maybe_fill(_N_CHUNK + n)
        qc = gq[rows]
        q_stack = jnp.concatenate(
            [jnp.where(lane_k == h, qc, jnp.zeros_like(qc)) for h in range(GLA_HEADS)], axis=0)
        o_all = lax.dot_general(q_stack, st, _NT, preferred_element_type=F32)
        for h in range(GLA_HEADS):
            o = _rms(o_all[h * CHUNK:(h + 1) * CHUNK, :], g_norm)
            gate = go[rows, h * GLA_DV:(h + 1) * GLA_DV]
            o_ref[0, rows, h * GLA_DV:(h + 1) * GLA_DV] = (
                o * (gate * jax.nn.sigmoid(gate))).astype(o_ref.dtype)
    for fill in fillers:
        fill()
    return state


def _mix_in_kernel(x_ref, g_ref, w_ref, fb_ref, w2_ref, b2_ref, gn_ref,
                   fq_ref, fk_ref, fv_ref, ka_ref, qa_ref, gla_ref, st_ref, off_ref):
    @pl.when(pl.program_id(1) == 0)
    def _():
        st_ref[...] = jnp.zeros_like(st_ref)
        off_ref[...] = jnp.zeros_like(off_ref)

    h = _rms_blocked(x_ref[0], g_ref[...])

    def proj(c0, width):
        return jnp.dot(h, w_ref[:, c0:c0 + width], preferred_element_type=F32)

    small = proj(_C_SM, SMALL_W)
    gk = proj(_C_GK, GLA_KW)
    gv = proj(_C_GV, GLA_VW).astype(BF16)
    gq = proj(_C_GQ, GLA_KW).astype(BF16)
    go = proj(_C_GO, GLA_VW)

    def fox_proj(out_ref, c0, col):
        def run():
            out_ref[0, :, col:col + _FOX_PIECE] = proj(c0 + col, _FOX_PIECE).astype(BF16)
        return run

    fillers = [fox_proj(ref, c0, col)
               for ref, c0 in ((fq_ref, _C_FQ), (fk_ref, _C_FK), (fv_ref, _C_FV))
               for col in range(0, FOX_WIDTH, _FOX_PIECE)]
    st_ref[...] = _gla_tile(gq, gk, gv, go, small, w2_ref[...], b2_ref[...], gn_ref[...],
                            st_ref[...], gla_ref, fillers)
    ka_ref[0], qa_ref[0], off_ref[...] = _decay_operands(small, fb_ref[...], off_ref[...])


def _mix_in(x1, g, w, fox_bias, w2, b2, gn):
    b, s, _ = x1.shape

    def rows(width):
        return pl.BlockSpec((1, TM_PROJ, width), lambda i, t: (i, t, 0))

    widths = (FOX_WIDTH, FOX_WIDTH, FOX_WIDTH, SMALL_W, SMALL_W, GLA_VW)
    return pl.pallas_call(
        _mix_in_kernel,
        out_shape=[jax.ShapeDtypeStruct((b, s, wd), BF16) for wd in widths],
        grid=(b, s // TM_PROJ),
        in_specs=[rows(D_MODEL), _const_spec((1, D_MODEL)), _const_spec((D_MODEL, PROJ_W)),
                  _const_spec((1, SMALL_W)), _const_spec((SMALL_W, GLA_KW)),
                  _const_spec((1, GLA_KW)), _const_spec((1, GLA_DV))],
        out_specs=[rows(wd) for wd in widths],
        scratch_shapes=[pltpu.VMEM((GLA_DV, GLA_KW), F32), pltpu.VMEM((1, SMALL_W), F32)],
        compiler_params=pltpu.CompilerParams(
            dimension_semantics=("arbitrary", "arbitrary"), vmem_limit_bytes=VMEM_LIMIT),
        name="mix_in",
    )(x1, g, w, fox_bias, w2, b2, gn)


_LOOKAHEAD = 2
N_PAIRS = FOX_HEADS // 2
TM_TAIL = 512


def _attention_pair(q_ref, k_ref, v_ref, qa_ref, ka_ref, head0, store, fillers):
    seq = q_ref.shape[1]
    lane = lax.broadcasted_iota(jnp.int32, (TQ, LANES), 1)
    k_idx = lax.broadcasted_iota(jnp.int32, (TQ, TQ), 0)
    q_idx = lax.broadcasted_iota(jnp.int32, (TQ, TQ), 1)
    causal = k_idx <= q_idx

    k_aug = jnp.concatenate([k_ref[0], ka_ref[0]], axis=1)
    v_t = v_ref[0].astype(F32).T.astype(BF16)

    def logits(qi):
        q0 = qi * TQ
        klen = q0 + TQ
        q = q_ref[0, q0:klen, :]
        qa = qa_ref[0, q0:klen, :]
        out = []
        for hh in range(2):
            in_head = (lane >= hh * FOX_DH) & (lane < (hh + 1) * FOX_DH)
            bias_lane = ((lane & (FOX_HEADS - 1)) == head0 + hh) & (lane < _FF_LANES)
            q_aug = jnp.concatenate([jnp.where(in_head, q, jnp.zeros_like(q)),
                                     jnp.where(bias_lane, qa, jnp.zeros_like(qa))], axis=1)
            out.append(lax.dot_general(k_aug[:klen], q_aug, _NT, preferred_element_type=F32))
        return out

    def finish(qi, s_both):
        q0 = qi * TQ
        klen = q0 + TQ
        outs = []
        for hh in range(2):
            s_t = s_both[hh]
            s_diag = jnp.where(causal, s_t[q0:], -jnp.inf)
            m = jnp.max(s_diag, axis=0, keepdims=True)
            if qi > 0:
                m = jnp.maximum(m, jnp.max(s_t[:q0], axis=0, keepdims=True))
            o_t, l = None, None
            for k0 in range(0, klen, TQ):
                p_t = jnp.exp2((s_diag if k0 == q0 else s_t[k0:k0 + TQ]) - m)
                l_t = jnp.sum(p_t, axis=0, keepdims=True)
                part = jnp.dot(v_t[hh * FOX_DH:(hh + 1) * FOX_DH, k0:k0 + TQ],
                               p_t.astype(BF16), preferred_element_type=F32)
                o_t = part if o_t is None else o_t + part
                l = l_t if l is None else l + l_t
            outs.append(o_t * (1.0 / l))
        store(q0, klen, jnp.concatenate(outs, axis=0).T.astype(BF16))

    n_blocks = seq // TQ
    order = list(range(1, n_blocks, 2)) + list(range(n_blocks - 2 + n_blocks % 2, -1, -2))
    fillers = list(fillers)
    per_block = -(-len(fillers) // n_blocks)
    ready = {qi: logits(qi) for qi in order[:_LOOKAHEAD]}
    for pos, qi in enumerate(order):
        if pos + _LOOKAHEAD < n_blocks:
            ahead = order[pos + _LOOKAHEAD]
            ready[ahead] = logits(ahead)
        finish(qi, ready.pop(qi))
        for fill in fillers[pos * per_block:(pos + 1) * per_block]:
            fill()


def _tail_thunks(x_ref, fox_rows, gla_ref, wo_ref, g_ref, wg_ref, wu_ref, wd_ref, gf_ref, o_ref):
    st = {"x2": [], "h": []}
    chunks = list(zip(FF_BOUNDS[:-1], FF_BOUNDS[1:]))
    thunks = []

    def head(rows):
        def run():
            mix = (jnp.dot(fox_rows(rows), wo_ref[:FOX_WIDTH, :], preferred_element_type=F32)
                   + jnp.dot(gla_ref[0, rows, :], wo_ref[FOX_WIDTH:, :],
                             preferred_element_type=F32))
            st["x2"].append(x_ref[0, rows, :] + mix)
            st["h"].append(_rms(st["x2"][-1], g_ref[...]).astype(BF16))
        return run

    def gate(c0, c1):
        def run():
            if "hcat" not in st:
                st["hcat"] = jnp.concatenate(st["h"], axis=0)
                st["x2cat"] = jnp.concatenate(st["x2"], axis=0)
            st["gate"] = jnp.dot(st["hcat"], wg_ref[:, c0:c1], preferred_element_type=F32)
        return run

    def up(c0, c1):
        def run():
            up_v = jnp.dot(st["hcat"], wu_ref[:, c0:c1], preferred_element_type=F32)
            st["act"] = (st["gate"] * jax.nn.sigmoid(st["gate"]) * up_v).astype(BF16)
        return run

    def down(c0, c1):
        def run():
            part = jnp.dot(st["act"], wd_ref[c0:c1, :], preferred_element_type=F32)
            st["y"] = part if "y" not in st else st["y"] + part
        return run

    def tail(c0, c1, rows):
        def run():
            part = jnp.dot(st["act"][rows], wd_ref[c0:c1, :], preferred_element_type=F32)
            y = part if "y" not in st else st["y"][rows] + part
            o_ref[0, rows, :] = _rms(st["x2cat"][rows] + 0.5 * y, gf_ref[...])
        return run

    for rows in _row_blocks(TM_TAIL, NORM_BLOCKS):
        thunks.append(head(rows))
    for c0, c1 in chunks:
        thunks.append(gate(c0, c1))
        thunks.append(up(c0, c1))
        if (c0, c1) != chunks[-1]:
            thunks.append(down(c0, c1))
    for rows in _row_blocks(TM_TAIL, TAIL_BLOCKS):
        thunks.append(tail(*chunks[-1], rows))
    return thunks


def _attn_first_kernel(q_ref, k_ref, v_ref, qa_ref, ka_ref, o_ref):
    def store(q0, klen, block):
        o_ref[0, q0:klen, :] = block

    _attention_pair(q_ref, k_ref, v_ref, qa_ref, ka_ref, 2 * pl.program_id(0), store, [])


def _attn_first(fq, fk, fv, qa, ka):
    _, s, _ = fq.shape
    qkv = pl.BlockSpec((1, s, LANES), lambda t: (0, 0, t))
    aug = pl.BlockSpec((1, s, SMALL_W), lambda t: (0, 0, 0))
    return pl.pallas_call(
        _attn_first_kernel,
        out_shape=jax.ShapeDtypeStruct((1, s, FOX_WIDTH), BF16),
        grid=(N_PAIRS,),
        in_specs=[qkv, qkv, qkv, aug, aug],
        out_specs=qkv,
        compiler_params=pltpu.CompilerParams(
            dimension_semantics=("arbitrary",), vmem_limit_bytes=VMEM_LIMIT),
        name="attn_first",
    )(fq, fk, fv, qa, ka)


def _attn_out_kernel(fox0_ref, q_ref, k_ref, v_ref, qa_ref, ka_ref, x_ref, gla_ref, wo_ref, g_ref,
                     wg_ref, wu_ref, wd_ref, gf_ref, o_ref, fox_scr):
    b = pl.program_id(0)
    t = pl.program_id(1)
    last = pl.num_programs(0) - 1
    read_slot = b % 2
    write_slot = 1 - read_slot
    row0 = t * TM_TAIL

    @pl.when((b == 0) & (t == 0))
    def _():
        for p in range(N_PAIRS):
            fox_scr[0, p] = fox0_ref[0, :, p * LANES:(p + 1) * LANES]

    def store(q0, klen, block):
        fox_scr[write_slot, t, q0:klen, :] = block

    def fox_rows(rows):
        n = rows.stop - rows.start
        start = pl.multiple_of(row0 + rows.start, n)
        return jnp.concatenate(
            [fox_scr[read_slot, p, pl.ds(start, n), :] for p in range(N_PAIRS)], axis=1)

    def tail_thunks():
        return _tail_thunks(x_ref, fox_rows, gla_ref, wo_ref, g_ref, wg_ref, wu_ref, wd_ref,
                            gf_ref, o_ref)

    def attention(fillers):
        _attention_pair(q_ref, k_ref, v_ref, qa_ref, ka_ref, 2 * t, store, fillers)

    @pl.when(b < last)
    def _():
        attention(tail_thunks())

    @pl.when(b == last)
    def _():
        for thunk in tail_thunks():
            thunk()


def _attn_out(fox0, fq, fk, fv, qa, ka, x1, gla, wo, g, wg, wu, wd, gf):
    b, s, _ = fq.shape
    assert s == N_PAIRS * TM_TAIL, "one tail tile per head pair"

    def ahead(width):
        return pl.BlockSpec((1, s, width), lambda i, t: (jnp.minimum(i + 1, b - 1), 0, t))

    aug = pl.BlockSpec((1, s, SMALL_W), lambda i, t: (jnp.minimum(i + 1, b - 1), 0, 0))

    def tile(width):
        return pl.BlockSpec((1, TM_TAIL, width), lambda i, t: (i, t, 0))

    return pl.pallas_call(
        _attn_out_kernel,
        out_shape=jax.ShapeDtypeStruct((b, s, D_MODEL), F32),
        grid=(b, N_PAIRS),
        in_specs=[_const_spec((1, s, FOX_WIDTH)), ahead(LANES), ahead(LANES), ahead(LANES), aug,
                  aug, tile(D_MODEL), tile(GLA_VW),
                  _const_spec((FOX_WIDTH + GLA_VW, D_MODEL)), _const_spec((1, D_MODEL)),
                  _const_spec((D_MODEL, D_FF)), _const_spec((D_MODEL, D_FF)),
                  _const_spec((D_FF, D_MODEL)), _const_spec((1, D_MODEL))],
        out_specs=tile(D_MODEL),
        scratch_shapes=[pltpu.VMEM((2, N_PAIRS, s, LANES), BF16)],
        compiler_params=pltpu.CompilerParams(
            dimension_semantics=("arbitrary", "arbitrary"), vmem_limit_bytes=VMEM_LIMIT),
        name="attn_out",
    )(fox0, fq, fk, fv, qa, ka, x1, gla, wo, g, wg, wu, wd, gf)


def _pack_w_in(w_in):
    widths = (FOX_WIDTH, FOX_WIDTH, FOX_WIDTH, FOX_HEADS, GLA_KW, GLA_KW, GLA_VW, GLA_RANK, GLA_VW)
    offs = [0]
    for wd in widths:
        offs.append(offs[-1] + wd)
    w_t = w_in.T
    fq, fk, fv, ff, gq, gk, gv, glow, gout = (w_t[offs[i]:offs[i + 1]] for i in range(9))
    pad = jnp.zeros((SMALL_W - _FF_LANES - GLA_RANK, D_MODEL), w_in.dtype)
    packed = jnp.concatenate(
        [fq * (_LOG2E * FOX_DH ** -0.5), fk, fv, gq * (GLA_DK ** -0.5), gk, gv, gout,
         jnp.tile(ff, (_FF_LANES // FOX_HEADS, 1)), glow, pad], axis=0)
    return packed.astype(BF16).T


def kernel(x, ffn1_norm, ffn1_w_gate, ffn1_w_up, ffn1_w_down, mix_norm, w_in, fox_forget_bias,
           gla_w_gate_up, gla_gate_bias, gla_out_norm, w_out, ffn2_norm, ffn2_w_gate, ffn2_w_up,
           ffn2_w_down, final_norm):
    b, s, d = x.shape
    n = b * s
    assert ffn1_norm.shape[0] == 1, "the final norm is fused into the single layer's last call"
    x = x.reshape(n, d)
    for l in range(1):
        x1 = _ffn1(x, ffn1_norm[l].reshape(1, d), ffn1_w_gate[l].astype(BF16),
                   ffn1_w_up[l].astype(BF16), ffn1_w_down[l].astype(BF16))

        bias = jnp.zeros((1, SMALL_W), F32).at[0, :_FF_LANES].set(
            jnp.tile(fox_forget_bias[l], _FF_LANES // FOX_HEADS))
        w2 = jnp.zeros((SMALL_W, GLA_KW), F32).at[_FF_LANES:_FF_LANES + GLA_RANK].set(
            gla_w_gate_up[l]).astype(BF16)
        fq, fk, fv, ka, qa, gla = _mix_in(
            x1.reshape(b, s, d), mix_norm[l].reshape(1, d), _pack_w_in(w_in[l]), bias, w2,
            gla_gate_bias[l].reshape(1, GLA_KW), gla_out_norm[l].reshape(1, GLA_DV))
        fox0 = _attn_first(fq, fk, fv, qa, ka)
        x = _attn_out(fox0, fq, fk, fv, qa, ka, x1.reshape(b, s, d), gla, w_out[l].astype(BF16),
                      ffn2_norm[l].reshape(1, d), ffn2_w_gate[l].astype(BF16),
                      ffn2_w_up[l].astype(BF16), ffn2_w_down[l].astype(BF16),
                      final_norm.reshape(1, d))
    return x
```

```python
import jax
import jax.numpy as jnp
from jax import lax
from jax.experimental import pallas as pl
from jax.experimental.pallas import tpu as pltpu

F32 = jnp.float32
BF16 = jnp.bfloat16

D_MODEL = 1024
D_FF = 2816
EPS = 1e-6
CHUNK = 64

FOX_HEADS = 8
FOX_DH = 64
FOX_WIDTH = FOX_HEADS * FOX_DH
GLA_HEADS = 4
GLA_DK = 64
GLA_DV = 128
GLA_KW = GLA_HEADS * GLA_DK
GLA_VW = GLA_HEADS * GLA_DV
GLA_RANK = 16
GLA_TEMP = 16.0

LANES = 128
SMALL_W = LANES
_C_FQ = 0
_C_FK = _C_FQ + FOX_WIDTH
_C_FV = _C_FK + FOX_WIDTH
_C_GQ = _C_FV + FOX_WIDTH
_C_GK = _C_GQ + GLA_KW
_C_GV = _C_GK + GLA_KW
_C_GO = _C_GV + GLA_VW
_C_SM = _C_GO + GLA_VW
PROJ_W = _C_SM + SMALL_W

TM_FFN = 1024
FF_BOUNDS = (0, 1024, 2048, D_FF)
NORM_BLOCKS = 4
TAIL_BLOCKS = 2
TM_PROJ = 1024
TQ = 256
VMEM_LIMIT = 56 * 1024 * 1024


def _rms(x, g):
    return x * lax.rsqrt(jnp.mean(x * x, axis=-1, keepdims=True) + EPS) * g


def _log_sigmoid(z):
    return jnp.minimum(z, 0.0) - jnp.log(1.0 + jnp.exp(-jnp.abs(z)))


def _row_blocks(n_rows, n_blocks):
    step = n_rows // n_blocks
    return [slice(r0, r0 + step) for r0 in range(0, n_rows, step)]


def _rms_blocked(x, g):
    return jnp.concatenate(
        [_rms(x[rows], g).astype(BF16) for rows in _row_blocks(x.shape[0], NORM_BLOCKS)], axis=0)


def _swiglu(h, wg_ref, wu_ref, wd_ref):
    y = None
    chunks = list(zip(FF_BOUNDS[:-1], FF_BOUNDS[1:]))
    for c0, c1 in chunks:
        gate = jnp.dot(h, wg_ref[:, c0:c1], preferred_element_type=F32)
        up = jnp.dot(h, wu_ref[:, c0:c1], preferred_element_type=F32)
        act = (gate * jax.nn.sigmoid(gate) * up).astype(BF16)
        if (c0, c1) != chunks[-1]:
            part = jnp.dot(act, wd_ref[c0:c1, :], preferred_element_type=F32)
            y = part if y is None else y + part
    out = []
    for rows in _row_blocks(h.shape[0], TAIL_BLOCKS):
        part = jnp.dot(act[rows], wd_ref[c0:c1, :], preferred_element_type=F32)
        out.append((rows, part if y is None else y[rows] + part))
    return out


def _segment_cumsum(x, seg):
    row = lax.broadcasted_iota(jnp.int32, x.shape, 0) & (seg - 1)
    shift = 1
    while shift < seg:
        x = x + jnp.where(row >= shift, pltpu.roll(x, shift, 0), 0.0)
        shift *= 2
    return x


def _split3(x):
    hi = x.astype(BF16)
    r1 = x - hi.astype(F32)
    mid = r1.astype(BF16)
    lo = (r1 - mid.astype(F32)).astype(BF16)
    return hi, mid, lo


def _const_spec(shape):
    n = len(shape)
    return pl.BlockSpec(shape, lambda *_: (0,) * n, pipeline_mode=pl.Buffered(1))


def _ffn1_kernel(x_ref, g_ref, wg_ref, wu_ref, wd_ref, o_ref):
    h = _rms_blocked(x_ref[...], g_ref[...])
    for rows, y in _swiglu(h, wg_ref, wu_ref, wd_ref):
        o_ref[rows, :] = x_ref[rows, :] + 0.5 * y


def _ffn1(x, g, wg, wu, wd):
    n = x.shape[0]
    row = pl.BlockSpec((TM_FFN, D_MODEL), lambda i: (i, 0))
    return pl.pallas_call(
        _ffn1_kernel,
        out_shape=jax.ShapeDtypeStruct((n, D_MODEL), F32),
        grid=(n // TM_FFN,),
        in_specs=[row, _const_spec((1, D_MODEL)), _const_spec((D_MODEL, D_FF)),
                  _const_spec((D_MODEL, D_FF)), _const_spec((D_FF, D_MODEL))],
        out_specs=row,
        compiler_params=pltpu.CompilerParams(
            dimension_semantics=("arbitrary",), vmem_limit_bytes=VMEM_LIMIT),
        name="ffn1",
    )(x, g, wg, wu, wd)


_DECAY_BLK = 256
_N_TERMS = 3
_BIAS_LANES = _N_TERMS * FOX_HEADS
_FF_LANES = 2 * _BIAS_LANES
_LOG2E = 1.4426950408889634
_NT = (((1,), (1,)), ((), ()))
_TN = (((0,), (0,)), ((), ()))
_N_CHUNK = TM_PROJ // CHUNK
_FOX_PIECE = 256
_DK_SHIFT = GLA_DK.bit_length() - 1


def _decay_operands(small, bias, offset):
    lane = lax.broadcasted_iota(jnp.int32, (_DECAY_BLK, SMALL_W), 1)
    term = (lane >> 3) % _N_TERMS
    lane_all = lax.broadcasted_iota(jnp.int32, small.shape, 1)
    log_f = jnp.where(lane_all < _FF_LANES, _log_sigmoid(small + bias), 0.0)
    local = _segment_cumsum(log_f, _DECAY_BLK)
    ka, qa = [], []
    for b0 in range(0, small.shape[0], _DECAY_BLK):
        cum = local[b0:b0 + _DECAY_BLK] + offset
        offset = cum[_DECAY_BLK - 1:_DECAY_BLK, :]
        hi, mid, lo = _split3(cum * _LOG2E)
        f_term = jnp.where(term == 0, hi.astype(F32),
                           jnp.where(term == 1, mid.astype(F32), lo.astype(F32)))
        one = jnp.ones_like(f_term)
        zero = jnp.zeros_like(f_term)
        ka.append(jnp.where(lane < _BIAS_LANES, -f_term,
                            jnp.where(lane < _FF_LANES, one, zero)).astype(BF16))
        qa.append(jnp.where(lane < _BIAS_LANES, one,
                            jnp.where(lane < _FF_LANES, f_term, zero)).astype(BF16))
    return jnp.concatenate(ka, axis=0), jnp.concatenate(qa, axis=0), offset


def _gla_tile(gq, gk, gv, go, small, w2, b2, g_norm, state, o_ref, fillers):
    fillers = list(fillers)
    every = max(1, (2 * _N_CHUNK) // max(1, len(fillers)))

    def maybe_fill(step):
        if fillers and step % every == every - 1:
            fillers.pop(0)()

    z = jnp.dot(small.astype(BF16), w2, preferred_element_type=F32) + b2
    log_a = _log_sigmoid(z) / GLA_TEMP

    g_cum_all = _segment_cumsum(log_a, CHUNK)
    chunks = [slice(n * CHUNK, (n + 1) * CHUNK) for n in range(_N_CHUNK)]
    lane_k = lax.broadcasted_iota(jnp.int32, (CHUNK, GLA_KW), 1) >> _DK_SHIFT

    deltas, decays = [], []
    for n, rows in enumerate(chunks):
        maybe_fill(n)
        g_cum = g_cum_all[rows]
        g_tot = g_cum[CHUNK - 1:CHUNK, :]
        k_dec = (gk[rows] * jnp.exp(g_tot - g_cum)).astype(BF16)
        delta = None
        for h in range(GLA_HEADS):
            part = lax.dot_general(gv[rows, h * GLA_DV:(h + 1) * GLA_DV],
                                   jnp.where(lane_k == h, k_dec, jnp.zeros_like(k_dec)), _TN,
                                   preferred_element_type=F32)
            delta = part if delta is None else delta + part
        deltas.append(delta)
        decays.append(jnp.exp(g_tot))

    states = []
    for delta, decay in zip(deltas, decays):
        state = decay * state + delta
        states.append(state.astype(BF16))

    for n, (rows, st) in enumerate(zip(chunks, states)):
        maybe_fill(_N_CHUNK + n)
        qc = gq[rows]
        q_stack = jnp.concatenate(
            [jnp.where(lane_k == h, qc, jnp.zeros_like(qc)) for h in range(GLA_HEADS)], axis=0)
        o_all = lax.dot_general(q_stack, st, _NT, preferred_element_type=F32)
        for h in range(GLA_HEADS):
            o = _rms(o_all[h * CHUNK:(h + 1) * CHUNK, :], g_norm)
            gate = go[rows, h * GLA_DV:(h + 1) * GLA_DV]
            o_ref[0, rows, h * GLA_DV:(h + 1) * GLA_DV] = (
                o * (gate * jax.nn.sigmoid(gate))).astype(o_ref.dtype)
    for fill in fillers:
        fill()
    return state


def _mix_in_kernel(x_ref, g_ref, w_ref, fb_ref, w2_ref, b2_ref, gn_ref,
                   fq_ref, fk_ref, fv_ref, ka_ref, qa_ref, gla_ref, st_ref, off_ref):
    @pl.when(pl.program_id(1) == 0)
    def _():
        st_ref[...] = jnp.zeros_like(st_ref)
        off_ref[...] = jnp.zeros_like(off_ref)

    h = _rms_blocked(x_ref[0], g_ref[...])

    def proj(c0, width):
        return jnp.dot(h, w_ref[:, c0:c0 + width], preferred_element_type=F32)

    small = proj(_C_SM, SMALL_W)
    gk = proj(_C_GK, GLA_KW)
    gv = proj(_C_GV, GLA_VW).astype(BF16)
    gq = proj(_C_GQ, GLA_KW).astype(BF16)
    go = proj(_C_GO, GLA_VW)

    def fox_proj(out_ref, c0, col):
        def run():
            out_ref[0, :, col:col + _FOX_PIECE] = proj(c0 + col, _FOX_PIECE).astype(BF16)
        return run

    fillers = [fox_proj(ref, c0, col)
               for ref, c0 in ((fq_ref, _C_FQ), (fk_ref, _C_FK), (fv_ref, _C_FV))
               for col in range(0, FOX_WIDTH, _FOX_PIECE)]
    st_ref[...] = _gla_tile(gq, gk, gv, go, small, w2_ref[...], b2_ref[...], gn_ref[...],
                            st_ref[...], gla_ref, fillers)
    ka_ref[0], qa_ref[0], off_ref[...] = _decay_operands(small, fb_ref[...], off_ref[...])


def _mix_in(x1, g, w, fox_bias, w2, b2, gn):
    b, s, _ = x1.shape

    def rows(width):
        return pl.BlockSpec((1, TM_PROJ, width), lambda i, t: (i, t, 0))

    widths = (FOX_WIDTH, FOX_WIDTH, FOX_WIDTH, SMALL_W, SMALL_W, GLA_VW)
    return pl.pallas_call(
        _mix_in_kernel,
        out_shape=[jax.ShapeDtypeStruct((b, s, wd), BF16) for wd in widths],
        grid=(b, s // TM_PROJ),
        in_specs=[rows(D_MODEL), _const_spec((1, D_MODEL)), _const_spec((D_MODEL, PROJ_W)),
                  _const_spec((1, SMALL_W)), _const_spec((SMALL_W, GLA_KW)),
                  _const_spec((1, GLA_KW)), _const_spec((1, GLA_DV))],
        out_specs=[rows(wd) for wd in widths],
        scratch_shapes=[pltpu.VMEM((GLA_DV, GLA_KW), F32), pltpu.VMEM((1, SMALL_W), F32)],
        compiler_params=pltpu.CompilerParams(
            dimension_semantics=("arbitrary", "arbitrary"), vmem_limit_bytes=VMEM_LIMIT),
        name="mix_in",
    )(x1, g, w, fox_bias, w2, b2, gn)


_LOOKAHEAD = 2
PAIRS_PER_STEP = 2

def _attn_kernel(q_ref, k_ref, v_ref, qa_ref, ka_ref, o_ref):
    seq = q_ref.shape[1]
    lane = lax.broadcasted_iota(jnp.int32, (TQ, LANES), 1)
    k_idx = lax.broadcasted_iota(jnp.int32, (TQ, TQ), 0)
    q_idx = lax.broadcasted_iota(jnp.int32, (TQ, TQ), 1)
    causal = k_idx <= q_idx
    pair_lanes = [slice(p * LANES, (p + 1) * LANES) for p in range(PAIRS_PER_STEP)]

    k_aug = [jnp.concatenate([k_ref[0, :, cols], ka_ref[0]], axis=1) for cols in pair_lanes]
    v_t = [v_ref[0, :, cols].astype(F32).T.astype(BF16) for cols in pair_lanes]

    def logits(p, qi):
        q0 = qi * TQ
        klen = q0 + TQ
        q = q_ref[0, q0:klen, pair_lanes[p]]
        qa = qa_ref[0, q0:klen, :]
        head0 = 2 * (pl.program_id(1) * PAIRS_PER_STEP + p)
        out = []
        for hh in range(2):
            in_head = (lane >= hh * FOX_DH) & (lane < (hh + 1) * FOX_DH)
            bias_lane = ((lane & (FOX_HEADS - 1)) == head0 + hh) & (lane < _FF_LANES)
            q_aug = jnp.concatenate([jnp.where(in_head, q, jnp.zeros_like(q)),
                                     jnp.where(bias_lane, qa, jnp.zeros_like(qa))], axis=1)
            out.append(lax.dot_general(k_aug[p][:klen], q_aug, _NT, preferred_element_type=F32))
        return out

    def finish(p, qi, s_both):
        q0 = qi * TQ
        klen = q0 + TQ
        outs = []
        for hh in range(2):
            s_t = s_both[hh]
            s_diag = jnp.where(causal, s_t[q0:], -jnp.inf)
            m = jnp.max(s_diag, axis=0, keepdims=True)
            if qi > 0:
                m = jnp.maximum(m, jnp.max(s_t[:q0], axis=0, keepdims=True))
            o_t, l = None, None
            for k0 in range(0, klen, TQ):
                p_t = jnp.exp2((s_diag if k0 == q0 else s_t[k0:k0 + TQ]) - m)
                l_t = jnp.sum(p_t, axis=0, keepdims=True)
                part = jnp.dot(v_t[p][hh * FOX_DH:(hh + 1) * FOX_DH, k0:k0 + TQ],
                               p_t.astype(BF16), preferred_element_type=F32)
                o_t = part if o_t is None else o_t + part
                l = l_t if l is None else l + l_t
            outs.append(o_t * (1.0 / l))
        o_ref[0, q0:klen, pair_lanes[p]] = jnp.concatenate(outs, axis=0).T.astype(o_ref.dtype)

    n_blocks = seq // TQ
    order = list(range(1, n_blocks, 2)) + list(range(n_blocks - 2 + n_blocks % 2, -1, -2))
    items = [(p, qi) for p in range(PAIRS_PER_STEP) for qi in order]
    ready = {item: logits(*item) for item in items[:_LOOKAHEAD]}
    for pos, item in enumerate(items):
        if pos + _LOOKAHEAD < len(items):
            ahead = items[pos + _LOOKAHEAD]
            ready[ahead] = logits(*ahead)
        finish(*item, ready.pop(item))


def _fox_attn(fq, fk, fv, qa, ka):
    b, s, _ = fq.shape
    qkv = pl.BlockSpec((1, s, PAIRS_PER_STEP * LANES), lambda i, h: (i, 0, h))
    aug = pl.BlockSpec((1, s, SMALL_W), lambda i, h: (i, 0, 0))
    return pl.pallas_call(
        _attn_kernel,
        out_shape=jax.ShapeDtypeStruct((b, s, FOX_WIDTH), BF16),
        grid=(b, FOX_HEADS // (2 * PAIRS_PER_STEP)),
        in_specs=[qkv, qkv, qkv, aug, aug],
        out_specs=qkv,
        compiler_params=pltpu.CompilerParams(
            dimension_semantics=("arbitrary", "arbitrary"), vmem_limit_bytes=VMEM_LIMIT),
        name="fox_attn",
    )(fq, fk, fv, qa, ka)


def _out_kernel(x_ref, fox_ref, gla_ref, wo_ref, g_ref, wg_ref, wu_ref, wd_ref, gf_ref, o_ref):
    x2, h = [], []
    for rows in _row_blocks(x_ref.shape[0], NORM_BLOCKS):
        mix = (jnp.dot(fox_ref[rows, :], wo_ref[:FOX_WIDTH, :], preferred_element_type=F32)
               + jnp.dot(gla_ref[rows, :], wo_ref[FOX_WIDTH:, :], preferred_element_type=F32))
        x2.append(x_ref[rows, :] + mix)
        h.append(_rms(x2[-1], g_ref[...]).astype(BF16))
    x2 = jnp.concatenate(x2, axis=0)
    for rows, y in _swiglu(jnp.concatenate(h, axis=0), wg_ref, wu_ref, wd_ref):
        o_ref[rows, :] = _rms(x2[rows] + 0.5 * y, gf_ref[...])


def _out_ffn2(x1, fox, gla, wo, g, wg, wu, wd, gf):
    n = x1.shape[0]

    def rows(width):
        return pl.BlockSpec((TM_FFN, width), lambda i: (i, 0))

    return pl.pallas_call(
        _out_kernel,
        out_shape=jax.ShapeDtypeStruct((n, D_MODEL), F32),
        grid=(n // TM_FFN,),
        in_specs=[rows(D_MODEL), rows(FOX_WIDTH), rows(GLA_VW),
                  _const_spec((FOX_WIDTH + GLA_VW, D_MODEL)), _const_spec((1, D_MODEL)),
                  _const_spec((D_MODEL, D_FF)), _const_spec((D_MODEL, D_FF)),
                  _const_spec((D_FF, D_MODEL)), _const_spec((1, D_MODEL))],
        out_specs=rows(D_MODEL),
        compiler_params=pltpu.CompilerParams(
            dimension_semantics=("arbitrary",), vmem_limit_bytes=VMEM_LIMIT),
        name="out_ffn2",
    )(x1, fox, gla, wo, g, wg, wu, wd, gf)


def _pack_w_in(w_in):
    widths = (FOX_WIDTH, FOX_WIDTH, FOX_WIDTH, FOX_HEADS, GLA_KW, GLA_KW, GLA_VW, GLA_RANK, GLA_VW)
    offs = [0]
    for wd in widths:
        offs.append(offs[-1] + wd)
    w_t = w_in.T
    fq, fk, fv, ff, gq, gk, gv, glow, gout = (w_t[offs[i]:offs[i + 1]] for i in range(9))
    pad = jnp.zeros((SMALL_W - _FF_LANES - GLA_RANK, D_MODEL), w_in.dtype)
    packed = jnp.concatenate(
        [fq * (_LOG2E * FOX_DH ** -0.5), fk, fv, gq * (GLA_DK ** -0.5), gk, gv, gout,
         jnp.tile(ff, (_FF_LANES // FOX_HEADS, 1)), glow, pad], axis=0)
    return packed.astype(BF16).T


def kernel(x, ffn1_norm, ffn1_w_gate, ffn1_w_up, ffn1_w_down, mix_norm, w_in, fox_forget_bias,
           gla_w_gate_up, gla_gate_bias, gla_out_norm, w_out, ffn2_norm, ffn2_w_gate, ffn2_w_up,
           ffn2_w_down, final_norm):
    b, s, d = x.shape
    n = b * s
    assert ffn1_norm.shape[0] == 1, "the final norm is fused into the single layer's last call"
    x = x.reshape(n, d)
    for l in range(1):
        x1 = _ffn1(x, ffn1_norm[l].reshape(1, d), ffn1_w_gate[l].astype(BF16),
                   ffn1_w_up[l].astype(BF16), ffn1_w_down[l].astype(BF16))

        bias = jnp.zeros((1, SMALL_W), F32).at[0, :_FF_LANES].set(
            jnp.tile(fox_forget_bias[l], _FF_LANES // FOX_HEADS))
        w2 = jnp.zeros((SMALL_W, GLA_KW), F32).at[_FF_LANES:_FF_LANES + GLA_RANK].set(
            gla_w_gate_up[l]).astype(BF16)
        fq, fk, fv, ka, qa, gla = _mix_in(
            x1.reshape(b, s, d), mix_norm[l].reshape(1, d), _pack_w_in(w_in[l]), bias, w2,
            gla_gate_bias[l].reshape(1, GLA_KW), gla_out_norm[l].reshape(1, GLA_DV))
        fox = _fox_attn(fq, fk, fv, qa, ka)

        x = _out_ffn2(x1, fox.reshape(n, -1), gla.reshape(n, -1), w_out[l].astype(BF16),
                      ffn2_norm[l].reshape(1, d), ffn2_w_gate[l].astype(BF16),
                      ffn2_w_up[l].astype(BF16), ffn2_w_down[l].astype(BF16),
                      final_norm.reshape(1, d))
    return x.reshape(b, s, d)
```

```python
import jax
import jax.numpy as jnp
from jax import lax
from jax.experimental import pallas as pl
from jax.experimental.pallas import tpu as pltpu

F32 = jnp.float32
BF16 = jnp.bfloat16

D_MODEL = 1024
D_FF = 2816
EPS = 1e-6
CHUNK = 64

FOX_HEADS = 8
FOX_DH = 64
FOX_WIDTH = FOX_HEADS * FOX_DH
GLA_HEADS = 4
GLA_DK = 64
GLA_DV = 128
GLA_KW = GLA_HEADS * GLA_DK
GLA_VW = GLA_HEADS * GLA_DV
GLA_RANK = 16
GLA_TEMP = 16.0

LANES = 128
SMALL_W = LANES
_C_FQ = 0
_C_FK = _C_FQ + FOX_WIDTH
_C_FV = _C_FK + FOX_WIDTH
_C_GQ = _C_FV + FOX_WIDTH
_C_GK = _C_GQ + GLA_KW
_C_GV = _C_GK + GLA_KW
_C_GO = _C_GV + GLA_VW
_C_SM = _C_GO + GLA_VW
PROJ_W = _C_SM + SMALL_W

TM_FFN = 1024
FF_BOUNDS = (0, 1024, 2048, D_FF)
NORM_BLOCKS = 4
TAIL_BLOCKS = 2
TM_PROJ = 1024
TQ = 256
VMEM_LIMIT = 56 * 1024 * 1024


def _rms(x, g):
    return x * lax.rsqrt(jnp.mean(x * x, axis=-1, keepdims=True) + EPS) * g


def _log_sigmoid(z):
    return jnp.minimum(z, 0.0) - jnp.log(1.0 + jnp.exp(-jnp.abs(z)))


def _row_blocks(n_rows, n_blocks):
    step = n_rows // n_blocks
    return [slice(r0, r0 + step) for r0 in range(0, n_rows, step)]


def _rms_blocked(x, g):
    return jnp.concatenate(
        [_rms(x[rows], g).astype(BF16) for rows in _row_blocks(x.shape[0], NORM_BLOCKS)], axis=0)


def _swiglu(h, wg_ref, wu_ref, wd_ref):
    y = None
    chunks = list(zip(FF_BOUNDS[:-1], FF_BOUNDS[1:]))
    for c0, c1 in chunks:
        gate = jnp.dot(h, wg_ref[:, c0:c1], preferred_element_type=F32)
        up = jnp.dot(h, wu_ref[:, c0:c1], preferred_element_type=F32)
        act = (gate * jax.nn.sigmoid(gate) * up).astype(BF16)
        if (c0, c1) != chunks[-1]:
            part = jnp.dot(act, wd_ref[c0:c1, :], preferred_element_type=F32)
            y = part if y is None else y + part
    out = []
    for rows in _row_blocks(h.shape[0], TAIL_BLOCKS):
        part = jnp.dot(act[rows], wd_ref[c0:c1, :], preferred_element_type=F32)
        out.append((rows, part if y is None else y[rows] + part))
    return out


def _segment_cumsum(x, seg):
    row = lax.broadcasted_iota(jnp.int32, x.shape, 0) & (seg - 1)
    shift = 1
    while shift < seg:
        x = x + jnp.where(row >= shift, pltpu.roll(x, shift, 0), 0.0)
        shift *= 2
    return x


def _split3(x):
    hi = x.astype(BF16)
    r1 = x - hi.astype(F32)
    mid = r1.astype(BF16)
    lo = (r1 - mid.astype(F32)).astype(BF16)
    return hi, mid, lo


def _const_spec(shape):
    n = len(shape)
    return pl.BlockSpec(shape, lambda *_: (0,) * n, pipeline_mode=pl.Buffered(1))


def _ffn1_kernel(x_ref, g_ref, wg_ref, wu_ref, wd_ref, o_ref):
    h = _rms_blocked(x_ref[...], g_ref[...])
    for rows, y in _swiglu(h, wg_ref, wu_ref, wd_ref):
        o_ref[rows, :] = x_ref[rows, :] + 0.5 * y


def _ffn1(x, g, wg, wu, wd):
    n = x.shape[0]
    row = pl.BlockSpec((TM_FFN, D_MODEL), lambda i: (i, 0))
    return pl.pallas_call(
        _ffn1_kernel,
        out_shape=jax.ShapeDtypeStruct((n, D_MODEL), F32),
        grid=(n // TM_FFN,),
        in_specs=[row, _const_spec((1, D_MODEL)), _const_spec((D_MODEL, D_FF)),
                  _const_spec((D_MODEL, D_FF)), _const_spec((D_FF, D_MODEL))],
        out_specs=row,
        compiler_params=pltpu.CompilerParams(
            dimension_semantics=("arbitrary",), vmem_limit_bytes=VMEM_LIMIT),
        name="ffn1",
    )(x, g, wg, wu, wd)


_DECAY_BLK = 256
_N_TERMS = 3
_BIAS_LANES = _N_TERMS * FOX_HEADS
_FF_LANES = 2 * _BIAS_LANES
_LOG2E = 1.4426950408889634
_NT = (((1,), (1,)), ((), ()))
_TN = (((0,), (0,)), ((), ()))
_N_CHUNK = TM_PROJ // CHUNK
_FOX_PIECE = 256
_DK_SHIFT = GLA_DK.bit_length() - 1


def _decay_operands(small, bias, offset):
    lane = lax.broadcasted_iota(jnp.int32, (_DECAY_BLK, SMALL_W), 1)
    term = (lane >> 3) % _N_TERMS
    lane_all = lax.broadcasted_iota(jnp.int32, small.shape, 1)
    log_f = jnp.where(lane_all < _FF_LANES, _log_sigmoid(small + bias), 0.0)
    local = _segment_cumsum(log_f, _DECAY_BLK)
    ka, qa = [], []
    for b0 in range(0, small.shape[0], _DECAY_BLK):
        cum = local[b0:b0 + _DECAY_BLK] + offset
        offset = cum[_DECAY_BLK - 1:_DECAY_BLK, :]
        hi, mid, lo = _split3(cum * _LOG2E)
        f_term = jnp.where(term == 0, hi.astype(F32),
                           jnp.where(term == 1, mid.astype(F32), lo.astype(F32)))
        one = jnp.ones_like(f_term)
        zero = jnp.zeros_like(f_term)
        ka.append(jnp.where(lane < _BIAS_LANES, -f_term,
                            jnp.where(lane < _FF_LANES, one, zero)).astype(BF16))
        qa.append(jnp.where(lane < _BIAS_LANES, one,
                            jnp.where(lane < _FF_LANES, f_term, zero)).astype(BF16))
    return jnp.concatenate(ka, axis=0), jnp.concatenate(qa, axis=0), offset


def _gla_tile(gq, gk, gv, go, small, w2, b2, g_norm, state, o_ref, fillers):
    fillers = list(fillers)
    every = max(1, (2 * _N_CHUNK) // max(1, len(fillers)))

    def maybe_fill(step):
        if fillers and step % every == every - 1:
            fillers.pop(0)()

    z = jnp.dot(small.astype(BF16), w2, preferred_element_type=F32) + b2
    log_a = _log_sigmoid(z) / GLA_TEMP

    g_cum_all = _segment_cumsum(log_a, CHUNK)
    chunks = [slice(n * CHUNK, (n + 1) * CHUNK) for n in range(_N_CHUNK)]
    lane_k = lax.broadcasted_iota(jnp.int32, (CHUNK, GLA_KW), 1) >> _DK_SHIFT

    g_tot_rows = jnp.concatenate(
        [g_cum_all[rows.stop - 1:rows.stop, :] for rows in chunks]
        + [jnp.zeros((LANES - _N_CHUNK, GLA_KW), F32)], axis=0)
    decay_cols = jnp.exp(g_tot_rows.T)

    deltas = []
    for n, rows in enumerate(chunks):
        maybe_fill(n)
        g_cum = g_cum_all[rows]
        k_dec = gk[rows] * jnp.exp(g_cum[CHUNK - 1:CHUNK, :] - g_cum)
        deltas.append(jnp.concatenate(
            [lax.dot_general(k_dec[:, h * GLA_DK:(h + 1) * GLA_DK].astype(BF16),
                             gv[rows, h * GLA_DV:(h + 1) * GLA_DV], _TN,
                             preferred_element_type=F32)
             for h in range(GLA_HEADS)], axis=0))

    states = []
    for n, delta in enumerate(deltas):
        state = decay_cols[:, n:n + 1] * state + delta
        states.append(state.astype(BF16))

    for n, (rows, st) in enumerate(zip(chunks, states)):
        maybe_fill(_N_CHUNK + n)
        qc = gq[rows]
        q_stack = jnp.concatenate(
            [jnp.where(lane_k == h, qc, jnp.zeros_like(qc)) for h in range(GLA_HEADS)], axis=0)
        o_all = jnp.dot(q_stack, st, preferred_element_type=F32)
        for h in range(GLA_HEADS):
            o = _rms(o_all[h * CHUNK:(h + 1) * CHUNK, :], g_norm)
            gate = go[rows, h * GLA_DV:(h + 1) * GLA_DV]
            o_ref[0, rows, h * GLA_DV:(h + 1) * GLA_DV] = (
                o * (gate * jax.nn.sigmoid(gate))).astype(o_ref.dtype)
    for fill in fillers:
        fill()
    return state


def _mix_in_kernel(x_ref, g_ref, w_ref, fb_ref, w2_ref, b2_ref, gn_ref,
                   fq_ref, fk_ref, fv_ref, ka_ref, qa_ref, gla_ref, st_ref, off_ref):
    @pl.when(pl.program_id(1) == 0)
    def _():
        st_ref[...] = jnp.zeros_like(st_ref)
        off_ref[...] = jnp.zeros_like(off_ref)

    h = _rms_blocked(x_ref[0], g_ref[...])

    def proj(c0, width):
        return jnp.dot(h, w_ref[:, c0:c0 + width], preferred_element_type=F32)

    small = proj(_C_SM, SMALL_W)
    gk = proj(_C_GK, GLA_KW)
    gv = proj(_C_GV, GLA_VW).astype(BF16)
    gq = proj(_C_GQ, GLA_KW).astype(BF16)
    go = proj(_C_GO, GLA_VW)

    def fox_proj(out_ref, c0, col):
        def run():
            out_ref[0, :, col:col + _FOX_PIECE] = proj(c0 + col, _FOX_PIECE).astype(BF16)
        return run

    fillers = [fox_proj(ref, c0, col)
               for ref, c0 in ((fq_ref, _C_FQ), (fk_ref, _C_FK), (fv_ref, _C_FV))
               for col in range(0, FOX_WIDTH, _FOX_PIECE)]
    st_ref[...] = _gla_tile(gq, gk, gv, go, small, w2_ref[...], b2_ref[...], gn_ref[...],
                            st_ref[...], gla_ref, fillers)
    ka_ref[0], qa_ref[0], off_ref[...] = _decay_operands(small, fb_ref[...], off_ref[...])


def _mix_in(x1, g, w, fox_bias, w2, b2, gn):
    b, s, _ = x1.shape

    def rows(width):
        return pl.BlockSpec((1, TM_PROJ, width), lambda i, t: (i, t, 0))

    widths = (FOX_WIDTH, FOX_WIDTH, FOX_WIDTH, SMALL_W, SMALL_W, GLA_VW)
    return pl.pallas_call(
        _mix_in_kernel,
        out_shape=[jax.ShapeDtypeStruct((b, s, wd), BF16) for wd in widths],
        grid=(b, s // TM_PROJ),
        in_specs=[rows(D_MODEL), _const_spec((1, D_MODEL)), _const_spec((D_MODEL, PROJ_W)),
                  _const_spec((1, SMALL_W)), _const_spec((SMALL_W, GLA_KW)),
                  _const_spec((1, GLA_KW)), _const_spec((1, GLA_DV))],
        out_specs=[rows(wd) for wd in widths],
        scratch_shapes=[pltpu.VMEM((GLA_KW, GLA_DV), F32), pltpu.VMEM((1, SMALL_W), F32)],
        compiler_params=pltpu.CompilerParams(
            dimension_semantics=("arbitrary", "arbitrary"), vmem_limit_bytes=VMEM_LIMIT),
        name="mix_in",
    )(x1, g, w, fox_bias, w2, b2, gn)


_LOOKAHEAD = 2
PAIRS_PER_STEP = 2

def _attn_kernel(q_ref, k_ref, v_ref, qa_ref, ka_ref, o_ref):
    seq = q_ref.shape[1]
    lane = lax.broadcasted_iota(jnp.int32, (TQ, LANES), 1)
    k_idx = lax.broadcasted_iota(jnp.int32, (TQ, TQ), 0)
    q_idx = lax.broadcasted_iota(jnp.int32, (TQ, TQ), 1)
    causal = k_idx <= q_idx
    pair_lanes = [slice(p * LANES, (p + 1) * LANES) for p in range(PAIRS_PER_STEP)]

    k_aug = [jnp.concatenate([k_ref[0, :, cols], ka_ref[0]], axis=1) for cols in pair_lanes]
    v_t = [v_ref[0, :, cols].T for cols in pair_lanes]

    def logits(p, qi):
        q0 = qi * TQ
        klen = q0 + TQ
        q = q_ref[0, q0:klen, pair_lanes[p]]
        qa = qa_ref[0, q0:klen, :]
        head0 = 2 * (pl.program_id(1) * PAIRS_PER_STEP + p)
        out = []
        for hh in range(2):
            in_head = (lane >= hh * FOX_DH) & (lane < (hh + 1) * FOX_DH)
            bias_lane = ((lane & (FOX_HEADS - 1)) == head0 + hh) & (lane < _FF_LANES)
            q_aug = jnp.concatenate([jnp.where(in_head, q, jnp.zeros_like(q)),
                                     jnp.where(bias_lane, qa, jnp.zeros_like(qa))], axis=1)
            out.append(lax.dot_general(k_aug[p][:klen], q_aug, _NT, preferred_element_type=F32))
        return out

    def finish(p, qi, s_both):
        q0 = qi * TQ
        klen = q0 + TQ
        outs = []
        for hh in range(2):
            s_t = s_both[hh]
            s_diag = jnp.where(causal, s_t[q0:], -jnp.inf)
            m = jnp.max(s_diag, axis=0, keepdims=True)
            if qi > 0:
                m = jnp.maximum(m, jnp.max(s_t[:q0], axis=0, keepdims=True))
            o_t, l = None, None
            for k0 in range(0, klen, TQ):
                p_t = jnp.exp2((s_diag if k0 == q0 else s_t[k0:k0 + TQ]) - m)
                l_t = jnp.sum(p_t, axis=0, keepdims=True)
                part = jnp.dot(v_t[p][hh * FOX_DH:(hh + 1) * FOX_DH, k0:k0 + TQ],
                               p_t.astype(BF16), preferred_element_type=F32)
                o_t = part if o_t is None else o_t + part
                l = l_t if l is None else l + l_t
            outs.append(o_t * (1.0 / l))
        o_ref[0, q0:klen, pair_lanes[p]] = jnp.concatenate(outs, axis=0).T.astype(o_ref.dtype)

    n_blocks = seq // TQ
    order = list(range(1, n_blocks, 2)) + list(range(n_blocks - 2 + n_blocks % 2, -1, -2))
    items = [(p, qi) for p in range(PAIRS_PER_STEP) for qi in order]
    ready = {item: logits(*item) for item in items[:_LOOKAHEAD]}
    for pos, item in enumerate(items):
        if pos + _LOOKAHEAD < len(items):
            ahead = items[pos + _LOOKAHEAD]
            ready[ahead] = logits(*ahead)
        finish(*item, ready.pop(item))


def _fox_attn(fq, fk, fv, qa, ka):
    b, s, _ = fq.shape
    qkv = pl.BlockSpec((1, s, PAIRS_PER_STEP * LANES), lambda i, h: (i, 0, h))
    aug = pl.BlockSpec((1, s, SMALL_W), lambda i, h: (i, 0, 0))
    return pl.pallas_call(
        _attn_kernel,
        out_shape=jax.ShapeDtypeStruct((b, s, FOX_WIDTH), BF16),
        grid=(b, FOX_HEADS // (2 * PAIRS_PER_STEP)),
        in_specs=[qkv, qkv, qkv, aug, aug],
        out_specs=qkv,
        compiler_params=pltpu.CompilerParams(
            dimension_semantics=("arbitrary", "arbitrary"), vmem_limit_bytes=VMEM_LIMIT),
        name="fox_attn",
    )(fq, fk, fv, qa, ka)


def _out_kernel(x_ref, fox_ref, gla_ref, wo_ref, g_ref, wg_ref, wu_ref, wd_ref, gf_ref, o_ref):
    x2, h = [], []
    for rows in _row_blocks(x_ref.shape[0], NORM_BLOCKS):
        mix = (jnp.dot(fox_ref[rows, :], wo_ref[:FOX_WIDTH, :], preferred_element_type=F32)
               + jnp.dot(gla_ref[rows, :], wo_ref[FOX_WIDTH:, :], preferred_element_type=F32))
        x2.append(x_ref[rows, :] + mix)
        h.append(_rms(x2[-1], g_ref[...]).astype(BF16))
    x2 = jnp.concatenate(x2, axis=0)
    for rows, y in _swiglu(jnp.concatenate(h, axis=0), wg_ref, wu_ref, wd_ref):
        o_ref[rows, :] = _rms(x2[rows] + 0.5 * y, gf_ref[...])


def _out_ffn2(x1, fox, gla, wo, g, wg, wu, wd, gf):
    n = x1.shape[0]

    def rows(width):
        return pl.BlockSpec((TM_FFN, width), lambda i: (i, 0))

    return pl.pallas_call(
        _out_kernel,
        out_shape=jax.ShapeDtypeStruct((n, D_MODEL), F32),
        grid=(n // TM_FFN,),
        in_specs=[rows(D_MODEL), rows(FOX_WIDTH), rows(GLA_VW),
                  _const_spec((FOX_WIDTH + GLA_VW, D_MODEL)), _const_spec((1, D_MODEL)),
                  _const_spec((D_MODEL, D_FF)), _const_spec((D_MODEL, D_FF)),
                  _const_spec((D_FF, D_MODEL)), _const_spec((1, D_MODEL))],
        out_specs=rows(D_MODEL),
        compiler_params=pltpu.CompilerParams(
            dimension_semantics=("arbitrary",), vmem_limit_bytes=VMEM_LIMIT),
        name="out_ffn2",
    )(x1, fox, gla, wo, g, wg, wu, wd, gf)


def _pack_w_in(w_in):
    widths = (FOX_WIDTH, FOX_WIDTH, FOX_WIDTH, FOX_HEADS, GLA_KW, GLA_KW, GLA_VW, GLA_RANK, GLA_VW)
    offs = [0]
    for wd in widths:
        offs.append(offs[-1] + wd)
    w_t = w_in.T
    fq, fk, fv, ff, gq, gk, gv, glow, gout = (w_t[offs[i]:offs[i + 1]] for i in range(9))
    pad = jnp.zeros((SMALL_W - _FF_LANES - GLA_RANK, D_MODEL), w_in.dtype)
    packed = jnp.concatenate(
        [fq * (_LOG2E * FOX_DH ** -0.5), fk, fv, gq * (GLA_DK ** -0.5), gk, gv, gout,
         jnp.tile(ff, (_FF_LANES // FOX_HEADS, 1)), glow, pad], axis=0)
    return packed.astype(BF16).T


def kernel(x, ffn1_norm, ffn1_w_gate, ffn1_w_up, ffn1_w_down, mix_norm, w_in, fox_forget_bias,
           gla_w_gate_up, gla_gate_bias, gla_out_norm, w_out, ffn2_norm, ffn2_w_gate, ffn2_w_up,
           ffn2_w_down, final_norm):
    b, s, d = x.shape
    n = b * s
    assert ffn1_norm.shape[0] == 1, "the final norm is fused into the single layer's last call"
    x = x.reshape(n, d)
    for l in range(1):
        x1 = _ffn1(x, ffn1_norm[l].reshape(1, d), ffn1_w_gate[l].astype(BF16),
                   ffn1_w_up[l].astype(BF16), ffn1_w_down[l].astype(BF16))

        bias = jnp.zeros((1, SMALL_W), F32).at[0, :_FF_LANES].set(
            jnp.tile(fox_forget_bias[l], _FF_LANES // FOX_HEADS))
        w2 = jnp.zeros((SMALL_W, GLA_KW), F32).at[_FF_LANES:_FF_LANES + GLA_RANK].set(
            gla_w_gate_up[l]).astype(BF16)
        fq, fk, fv, ka, qa, gla = _mix_in(
            x1.reshape(b, s, d), mix_norm[l].reshape(1, d), _pack_w_in(w_in[l]), bias, w2,
            gla_gate_bias[l].reshape(1, GLA_KW), gla_out_norm[l].reshape(1, GLA_DV))
        fox = _fox_attn(fq, fk, fv, qa, ka)

        x = _out_ffn2(x1, fox.reshape(n, -1), gla.reshape(n, -1), w_out[l].astype(BF16),
                      ffn2_norm[l].reshape(1, d), ffn2_w_gate[l].astype(BF16),
                      ffn2_w_up[l].astype(BF16), ffn2_w_down[l].astype(BF16),
                      final_norm.reshape(1, d))
    return x.reshape(b, s, d)
```

```python
import jax
import jax.numpy as jnp
from jax import lax
from jax.experimental import pallas as pl
from jax.experimental.pallas import tpu as pltpu

F32 = jnp.float32
BF16 = jnp.bfloat16

D_MODEL = 1024
D_FF = 2816
EPS = 1e-6
CHUNK = 64

FOX_HEADS = 8
FOX_DH = 64
FOX_WIDTH = FOX_HEADS * FOX_DH
GLA_HEADS = 4
GLA_DK = 64
GLA_DV = 128
GLA_KW = GLA_HEADS * GLA_DK
GLA_VW = GLA_HEADS * GLA_DV
GLA_RANK = 16
GLA_TEMP = 16.0

LANES = 128
SMALL_W = LANES
_C_FQ = 0
_C_FK = _C_FQ + FOX_WIDTH
_C_FV = _C_FK + FOX_WIDTH
_C_GQ = _C_FV + FOX_WIDTH
_C_GK = _C_GQ + GLA_KW
_C_GV = _C_GK + GLA_KW
_C_GO = _C_GV + GLA_VW
_C_SM = _C_GO + GLA_VW
PROJ_W = _C_SM + SMALL_W
_IN_OFFS = (0,)
for _w in (FOX_WIDTH, FOX_WIDTH, FOX_WIDTH, FOX_HEADS, GLA_KW, GLA_KW, GLA_VW, GLA_RANK, GLA_VW):
    _IN_OFFS += (_IN_OFFS[-1] + _w,)

TM_FFN = 1024
FF_BOUNDS = (0, 1024, 2048, D_FF)
NORM_BLOCKS = 4
TAIL_BLOCKS = 2
TM_PROJ = 1024
TQ = 256
VMEM_LIMIT = 56 * 1024 * 1024


def _rms(x, g):
    return x * lax.rsqrt(jnp.mean(x * x, axis=-1, keepdims=True) + EPS) * g


def _log_sigmoid(z):
    return jnp.minimum(z, 0.0) - jnp.log(1.0 + jnp.exp(-jnp.abs(z)))


def _row_blocks(n_rows, n_blocks):
    step = n_rows // n_blocks
    return [slice(r0, r0 + step) for r0 in range(0, n_rows, step)]


def _rms_blocked(x, g):
    return jnp.concatenate(
        [_rms(x[rows], g).astype(BF16) for rows in _row_blocks(x.shape[0], NORM_BLOCKS)], axis=0)


def _swiglu(h, wg_ref, wu_ref, wd_ref):
    y = None
    chunks = list(zip(FF_BOUNDS[:-1], FF_BOUNDS[1:]))
    for c0, c1 in chunks:
        gate = jnp.dot(h, wg_ref[:, c0:c1], preferred_element_type=F32)
        up = jnp.dot(h, wu_ref[:, c0:c1], preferred_element_type=F32)
        act = (gate * jax.nn.sigmoid(gate) * up).astype(BF16)
        if (c0, c1) != chunks[-1]:
            part = jnp.dot(act, wd_ref[c0:c1, :], preferred_element_type=F32)
            y = part if y is None else y + part
    out = []
    for rows in _row_blocks(h.shape[0], TAIL_BLOCKS):
        part = jnp.dot(act[rows], wd_ref[c0:c1, :], preferred_element_type=F32)
        out.append((rows, part if y is None else y[rows] + part))
    return out


def _segment_cumsum(x, seg):
    row = lax.broadcasted_iota(jnp.int32, x.shape, 0) & (seg - 1)
    shift = 1
    while shift < seg:
        x = x + jnp.where(row >= shift, pltpu.roll(x, shift, 0), 0.0)
        shift *= 2
    return x


def _split3(x):
    hi = x.astype(BF16)
    r1 = x - hi.astype(F32)
    mid = r1.astype(BF16)
    lo = (r1 - mid.astype(F32)).astype(BF16)
    return hi, mid, lo


def _const_spec(shape):
    n = len(shape)
    return pl.BlockSpec(shape, lambda *_: (0,) * n, pipeline_mode=pl.Buffered(1))


def _ffn1_kernel(x_ref, g_ref, wg_ref, wu_ref, wd_ref, o_ref):
    h = _rms_blocked(x_ref[...], g_ref[...])
    for rows, y in _swiglu(h, wg_ref, wu_ref, wd_ref):
        o_ref[rows, :] = x_ref[rows, :] + 0.5 * y


def _ffn1(x, g, wg, wu, wd):
    n = x.shape[0]
    row = pl.BlockSpec((TM_FFN, D_MODEL), lambda i: (i, 0))
    return pl.pallas_call(
        _ffn1_kernel,
        out_shape=jax.ShapeDtypeStruct((n, D_MODEL), F32),
        grid=(n // TM_FFN,),
        in_specs=[row, _const_spec((1, D_MODEL)), _const_spec((D_MODEL, D_FF)),
                  _const_spec((D_MODEL, D_FF)), _const_spec((D_FF, D_MODEL))],
        out_specs=row,
        compiler_params=pltpu.CompilerParams(
            dimension_semantics=("arbitrary",), vmem_limit_bytes=VMEM_LIMIT),
        name="ffn1",
    )(x, g, wg, wu, wd)


_DECAY_BLK = 256
_N_TERMS = 3
_BIAS_LANES = _N_TERMS * FOX_HEADS
_FF_LANES = 2 * _BIAS_LANES
_LOG2E = 1.4426950408889634
_NT = (((1,), (1,)), ((), ()))
_TN = (((0,), (0,)), ((), ()))
_N_CHUNK = TM_PROJ // CHUNK
_FOX_PIECE = 256
_EARLY_FILLERS = 2
_DK_SHIFT = GLA_DK.bit_length() - 1
_DV_SHIFT = GLA_DV.bit_length() - 1


def _decay_operands(small, bias, offset):
    lane = lax.broadcasted_iota(jnp.int32, (_DECAY_BLK, SMALL_W), 1)
    term = (lane >> 3) % _N_TERMS
    lane_all = lax.broadcasted_iota(jnp.int32, small.shape, 1)
    log_f = jnp.where(lane_all < _FF_LANES, _log_sigmoid(small + bias), 0.0)
    local = _segment_cumsum(log_f, _DECAY_BLK)
    ka, qa = [], []
    for b0 in range(0, small.shape[0], _DECAY_BLK):
        cum = local[b0:b0 + _DECAY_BLK] + offset
        offset = cum[_DECAY_BLK - 1:_DECAY_BLK, :]
        hi, mid, lo = _split3(cum * _LOG2E)
        f_term = jnp.where(term == 0, hi.astype(F32),
                           jnp.where(term == 1, mid.astype(F32), lo.astype(F32)))
        one = jnp.ones_like(f_term)
        zero = jnp.zeros_like(f_term)
        ka.append(jnp.where(lane < _BIAS_LANES, -f_term,
                            jnp.where(lane < _FF_LANES, one, zero)).astype(BF16))
        qa.append(jnp.where(lane < _BIAS_LANES, one,
                            jnp.where(lane < _FF_LANES, f_term, zero)).astype(BF16))
    return jnp.concatenate(ka, axis=0), jnp.concatenate(qa, axis=0), offset


def _gla_tile(gq, gk, gv, go, small, w2, b2, g_norm, state, o_ref, fillers):
    fillers = list(fillers)
    z = jnp.dot(small.astype(BF16), w2, preferred_element_type=F32) + b2
    for _ in range(min(_EARLY_FILLERS, len(fillers))):
        fillers.pop(0)()
    every = max(1, (2 * _N_CHUNK) // max(1, len(fillers)))

    def maybe_fill(step):
        if fillers and step % every == every - 1:
            fillers.pop(0)()

    log_a = _log_sigmoid(z) / GLA_TEMP

    g_cum_all = _segment_cumsum(log_a, CHUNK)
    chunks = [slice(n * CHUNK, (n + 1) * CHUNK) for n in range(_N_CHUNK)]
    bd_mask = ((lax.broadcasted_iota(jnp.int32, (GLA_KW, GLA_VW), 0) >> _DK_SHIFT)
               == (lax.broadcasted_iota(jnp.int32, (GLA_KW, GLA_VW), 1) >> _DV_SHIFT))

    g_tot_rows = jnp.concatenate(
        [g_cum_all[rows.stop - 1:rows.stop, :] for rows in chunks]
        + [jnp.zeros((LANES - _N_CHUNK, GLA_KW), F32)], axis=0)
    decay_cols = jnp.exp(g_tot_rows.T)

    deltas = []
    for n, rows in enumerate(chunks):
        maybe_fill(n)
        g_cum = g_cum_all[rows]
        k_dec = gk[rows] * jnp.exp(g_cum[CHUNK - 1:CHUNK, :] - g_cum)
        deltas.append(jnp.concatenate(
            [lax.dot_general(k_dec[:, h * GLA_DK:(h + 1) * GLA_DK].astype(BF16),
                             gv[rows, h * GLA_DV:(h + 1) * GLA_DV], _TN,
                             preferred_element_type=F32)
             for h in range(GLA_HEADS)], axis=0))

    states = []
    for n, delta in enumerate(deltas):
        state = decay_cols[:, n:n + 1] * state + delta
        states.append(state.astype(BF16))

    for n, (rows, st) in enumerate(zip(chunks, states)):
        maybe_fill(_N_CHUNK + n)
        st_bd = jnp.where(bd_mask, jnp.concatenate([st] * GLA_HEADS, axis=1), jnp.zeros((), BF16))
        o_all = jnp.dot(gq[rows], st_bd, preferred_element_type=F32)
        for h in range(GLA_HEADS):
            o = _rms(o_all[:, h * GLA_DV:(h + 1) * GLA_DV], g_norm)
            gate = go[rows, h * GLA_DV:(h + 1) * GLA_DV]
            o_ref[0, rows, h * GLA_DV:(h + 1) * GLA_DV] = (
                o * (gate * jax.nn.sigmoid(gate))).astype(o_ref.dtype)
    for fill in fillers:
        fill()
    return state


def _pack_w_in(wt_ref, w_ref):
    def put(c0, rows, scale=None):
        blk = wt_ref[rows, :]
        if scale is not None:
            blk = blk * scale
        w_ref[:, c0:c0 + blk.shape[0]] = blk.T.astype(BF16)

    def src(i):
        return slice(_IN_OFFS[i], _IN_OFFS[i + 1])

    put(_C_FQ, src(0), _LOG2E * FOX_DH ** -0.5)
    put(_C_FK, src(1))
    put(_C_FV, src(2))
    put(_C_GQ, src(4), GLA_DK ** -0.5)
    put(_C_GK, src(5))
    put(_C_GV, src(6))
    put(_C_GO, src(8))
    small = jnp.concatenate(
        [wt_ref[src(3), :]] * (_FF_LANES // FOX_HEADS) + [wt_ref[src(7), :]]
        + [jnp.zeros((SMALL_W - _FF_LANES - GLA_RANK, D_MODEL), F32)], axis=0)
    w_ref[:, _C_SM:] = small.T.astype(BF16)


def _mix_in_kernel(x_ref, g_ref, wt_ref, fb_ref, w2_ref, b2_ref, gn_ref,
                   fq_ref, fk_ref, fv_ref, ka_ref, qa_ref, gla_ref, st_ref, off_ref, w_ref):
    @pl.when((pl.program_id(0) == 0) & (pl.program_id(1) == 0))
    def _():
        _pack_w_in(wt_ref, w_ref)

    @pl.when(pl.program_id(1) == 0)
    def _():
        st_ref[...] = jnp.zeros_like(st_ref)
        off_ref[...] = jnp.zeros_like(off_ref)

    h = _rms_blocked(x_ref[0], g_ref[...])

    def proj(c0, width):
        return jnp.dot(h, w_ref[:, c0:c0 + width], preferred_element_type=F32)

    small = proj(_C_SM, SMALL_W)
    gk = proj(_C_GK, GLA_KW)
    gv = proj(_C_GV, GLA_VW).astype(BF16)
    gq = proj(_C_GQ, GLA_KW).astype(BF16)
    go = proj(_C_GO, GLA_VW)

    def fox_proj(out_ref, c0, col):
        def run():
            out_ref[0, :, col:col + _FOX_PIECE] = proj(c0 + col, _FOX_PIECE).astype(BF16)
        return run

    fillers = [fox_proj(ref, c0, col)
               for ref, c0 in ((fq_ref, _C_FQ), (fk_ref, _C_FK), (fv_ref, _C_FV))
               for col in range(0, FOX_WIDTH, _FOX_PIECE)]
    st_ref[...] = _gla_tile(gq, gk, gv, go, small, w2_ref[...], b2_ref[...], gn_ref[...],
                            st_ref[...], gla_ref, fillers)
    ka_ref[0], qa_ref[0], off_ref[...] = _decay_operands(small, fb_ref[...], off_ref[...])


def _mix_in(x1, g, w_t, fox_bias, w2, b2, gn):
    b, s, _ = x1.shape
    assert w_t.shape == (_IN_OFFS[-1], D_MODEL)

    def rows(width):
        return pl.BlockSpec((1, TM_PROJ, width), lambda i, t: (i, t, 0))

    widths = (FOX_WIDTH, FOX_WIDTH, FOX_WIDTH, SMALL_W, SMALL_W, GLA_VW)
    return pl.pallas_call(
        _mix_in_kernel,
        out_shape=[jax.ShapeDtypeStruct((b, s, wd), BF16) for wd in widths],
        grid=(b, s // TM_PROJ),
        in_specs=[rows(D_MODEL), _const_spec((1, D_MODEL)), _const_spec(w_t.shape),
                  _const_spec((1, SMALL_W)), _const_spec((SMALL_W, GLA_KW)),
                  _const_spec((1, GLA_KW)), _const_spec((1, GLA_DV))],
        out_specs=[rows(wd) for wd in widths],
        scratch_shapes=[pltpu.VMEM((GLA_KW, GLA_DV), F32), pltpu.VMEM((1, SMALL_W), F32),
                        pltpu.VMEM((D_MODEL, PROJ_W), BF16)],
        compiler_params=pltpu.CompilerParams(
            dimension_semantics=("arbitrary", "arbitrary"), vmem_limit_bytes=VMEM_LIMIT),
        name="mix_in",
    )(x1, g, w_t, fox_bias, w2, b2, gn)


_LOOKAHEAD = 2
PAIRS_PER_STEP = 2

def _attn_kernel(q_ref, k_ref, v_ref, qa_ref, ka_ref, o_ref):
    seq = q_ref.shape[1]
    lane = lax.broadcasted_iota(jnp.int32, (TQ, LANES), 1)
    k_idx = lax.broadcasted_iota(jnp.int32, (TQ, TQ), 0)
    q_idx = lax.broadcasted_iota(jnp.int32, (TQ, TQ), 1)
    causal = k_idx <= q_idx
    pair_lanes = [slice(p * LANES, (p + 1) * LANES) for p in range(PAIRS_PER_STEP)]

    k_aug = [jnp.concatenate([k_ref[0, :, cols], ka_ref[0]], axis=1) for cols in pair_lanes]
    v_t = [v_ref[0, :, cols].T for cols in pair_lanes]

    def logits(p, qi):
        q0 = qi * TQ
        klen = q0 + TQ
        q = q_ref[0, q0:klen, pair_lanes[p]]
        qa = qa_ref[0, q0:klen, :]
        head0 = 2 * (pl.program_id(1) * PAIRS_PER_STEP + p)
        out = []
        for hh in range(2):
            in_head = (lane >= hh * FOX_DH) & (lane < (hh + 1) * FOX_DH)
            bias_lane = ((lane & (FOX_HEADS - 1)) == head0 + hh) & (lane < _FF_LANES)
            q_aug = jnp.concatenate([jnp.where(in_head, q, jnp.zeros_like(q)),
                                     jnp.where(bias_lane, qa, jnp.zeros_like(qa))], axis=1)
            out.append(lax.dot_general(k_aug[p][:klen], q_aug, _NT, preferred_element_type=F32))
        return out

    def finish(p, qi, s_both):
        q0 = qi * TQ
        klen = q0 + TQ
        outs = []
        for hh in range(2):
            s_t = s_both[hh]
            s_diag = jnp.where(causal, s_t[q0:], -jnp.inf)
            m = jnp.max(s_diag, axis=0, keepdims=True)
            if qi > 0:
                m = jnp.maximum(m, jnp.max(s_t[:q0], axis=0, keepdims=True))
            o_t, l = None, None
            for k0 in range(0, klen, TQ):
                p_t = jnp.exp2((s_diag if k0 == q0 else s_t[k0:k0 + TQ]) - m)
                l_t = jnp.sum(p_t, axis=0, keepdims=True)
                part = jnp.dot(v_t[p][hh * FOX_DH:(hh + 1) * FOX_DH, k0:k0 + TQ],
                               p_t.astype(BF16), preferred_element_type=F32)
                o_t = part if o_t is None else o_t + part
                l = l_t if l is None else l + l_t
            outs.append(o_t * (1.0 / l))
        o_ref[0, q0:klen, pair_lanes[p]] = jnp.concatenate(outs, axis=0).T.astype(o_ref.dtype)

    n_blocks = seq // TQ
    order = list(range(1, n_blocks, 2)) + list(range(n_blocks - 2 + n_blocks % 2, -1, -2))
    items = [(p, qi) for p in range(PAIRS_PER_STEP) for qi in order]
    ready = {item: logits(*item) for item in items[:_LOOKAHEAD]}
    for pos, item in enumerate(items):
        if pos + _LOOKAHEAD < len(items):
            ahead = items[pos + _LOOKAHEAD]
            ready[ahead] = logits(*ahead)
        finish(*item, ready.pop(item))


def _fox_attn(fq, fk, fv, qa, ka):
    b, s, _ = fq.shape
    qkv = pl.BlockSpec((1, s, PAIRS_PER_STEP * LANES), lambda i, h: (i, 0, h))
    aug = pl.BlockSpec((1, s, SMALL_W), lambda i, h: (i, 0, 0))
    return pl.pallas_call(
        _attn_kernel,
        out_shape=jax.ShapeDtypeStruct((b, s, FOX_WIDTH), BF16),
        grid=(b, FOX_HEADS // (2 * PAIRS_PER_STEP)),
        in_specs=[qkv, qkv, qkv, aug, aug],
        out_specs=qkv,
        compiler_params=pltpu.CompilerParams(
            dimension_semantics=("arbitrary", "arbitrary"), vmem_limit_bytes=VMEM_LIMIT),
        name="fox_attn",
    )(fq, fk, fv, qa, ka)


def _out_kernel(x_ref, fox_ref, gla_ref, wo_ref, g_ref, wg_ref, wu_ref, wd_ref, gf_ref, o_ref):
    x2, h = [], []
    for rows in _row_blocks(x_ref.shape[0], NORM_BLOCKS):
        mix = (jnp.dot(fox_ref[rows, :], wo_ref[:FOX_WIDTH, :], preferred_element_type=F32)
               + jnp.dot(gla_ref[rows, :], wo_ref[FOX_WIDTH:, :], preferred_element_type=F32))
        x2.append(x_ref[rows, :] + mix)
        h.append(_rms(x2[-1], g_ref[...]).astype(BF16))
    x2 = jnp.concatenate(x2, axis=0)
    for rows, y in _swiglu(jnp.concatenate(h, axis=0), wg_ref, wu_ref, wd_ref):
        o_ref[rows, :] = _rms(x2[rows] + 0.5 * y, gf_ref[...])


def _out_ffn2(x1, fox, gla, wo, g, wg, wu, wd, gf):
    n = x1.shape[0]

    def rows(width):
        return pl.BlockSpec((TM_FFN, width), lambda i: (i, 0))

    return pl.pallas_call(
        _out_kernel,
        out_shape=jax.ShapeDtypeStruct((n, D_MODEL), F32),
        grid=(n // TM_FFN,),
        in_specs=[rows(D_MODEL), rows(FOX_WIDTH), rows(GLA_VW),
                  _const_spec((FOX_WIDTH + GLA_VW, D_MODEL)), _const_spec((1, D_MODEL)),
                  _const_spec((D_MODEL, D_FF)), _const_spec((D_MODEL, D_FF)),
                  _const_spec((D_FF, D_MODEL)), _const_spec((1, D_MODEL))],
        out_specs=rows(D_MODEL),
        compiler_params=pltpu.CompilerParams(
            dimension_semantics=("arbitrary",), vmem_limit_bytes=VMEM_LIMIT),
        name="out_ffn2",
    )(x1, fox, gla, wo, g, wg, wu, wd, gf)


def kernel(x, ffn1_norm, ffn1_w_gate, ffn1_w_up, ffn1_w_down, mix_norm, w_in, fox_forget_bias,
           gla_w_gate_up, gla_gate_bias, gla_out_norm, w_out, ffn2_norm, ffn2_w_gate, ffn2_w_up,
           ffn2_w_down, final_norm):
    b, s, d = x.shape
    n = b * s
    assert ffn1_norm.shape[0] == 1, "the final norm is fused into the single layer's last call"
    x = x.reshape(n, d)
    for l in range(1):
        x1 = _ffn1(x, ffn1_norm[l].reshape(1, d), ffn1_w_gate[l].astype(BF16),
                   ffn1_w_up[l].astype(BF16), ffn1_w_down[l].astype(BF16))

        bias = jnp.zeros((1, SMALL_W), F32).at[0, :_FF_LANES].set(
            jnp.tile(fox_forget_bias[l], _FF_LANES // FOX_HEADS))
        w2 = jnp.zeros((SMALL_W, GLA_KW), F32).at[_FF_LANES:_FF_LANES + GLA_RANK].set(
            gla_w_gate_up[l]).astype(BF16)
        fq, fk, fv, ka, qa, gla = _mix_in(
            x1.reshape(b, s, d), mix_norm[l].reshape(1, d), w_in[l].T, bias, w2,
            gla_gate_bias[l].reshape(1, GLA_KW), gla_out_norm[l].reshape(1, GLA_DV))
        fox = _fox_attn(fq, fk, fv, qa, ka)

        x = _out_ffn2(x1, fox.reshape(n, -1), gla.reshape(n, -1), w_out[l].astype(BF16),
                      ffn2_norm[l].reshape(1, d), ffn2_w_gate[l].astype(BF16),
                      ffn2_w_up[l].astype(BF16), ffn2_w_down[l].astype(BF16),
                      final_norm.reshape(1, d))
    return x.reshape(b, s, d)
```

```python
import jax
import jax.numpy as jnp
from jax import lax
from jax.experimental import pallas as pl
from jax.experimental.pallas import tpu as pltpu

F32 = jnp.float32
BF16 = jnp.bfloat16

D_MODEL = 1024
D_FF = 2816
EPS = 1e-6
CHUNK = 64

FOX_HEADS = 8
FOX_DH = 64
FOX_WIDTH = FOX_HEADS * FOX_DH
GLA_HEADS = 4
GLA_DK = 64
GLA_DV = 128
GLA_KW = GLA_HEADS * GLA_DK
GLA_VW = GLA_HEADS * GLA_DV
GLA_RANK = 16
GLA_TEMP = 16.0

LANES = 128
BF16_SUBLANES = 16
SMALL_W = LANES
_C_FQ = 0
_C_FK = _C_FQ + FOX_WIDTH
_C_FV = _C_FK + FOX_WIDTH
_C_GQ = _C_FV + FOX_WIDTH
_C_GK = _C_GQ + GLA_KW
_C_GV = _C_GK + GLA_KW
_C_GO = _C_GV + GLA_VW
_C_SM = _C_GO + GLA_VW
PROJ_W = _C_SM + SMALL_W
_IN_OFFS = (0,)
for _w in (FOX_WIDTH, FOX_WIDTH, FOX_WIDTH, FOX_HEADS, GLA_KW, GLA_KW, GLA_VW, GLA_RANK, GLA_VW):
    _IN_OFFS += (_IN_OFFS[-1] + _w,)

TM_FFN = 1024
FF_BOUNDS = (0, 1024, 2048, D_FF)
NORM_BLOCKS = 4
TAIL_BLOCKS = 2
TM_PROJ = 1024
TQ = 256
VMEM_LIMIT = 56 * 1024 * 1024


def _rms(x, g):
    return x * lax.rsqrt(jnp.mean(x * x, axis=-1, keepdims=True) + EPS) * g


def _log_sigmoid(z):
    return jnp.minimum(z, 0.0) - jnp.log(1.0 + jnp.exp(-jnp.abs(z)))


def _row_blocks(n_rows, n_blocks):
    step = n_rows // n_blocks
    return [slice(r0, r0 + step) for r0 in range(0, n_rows, step)]


def _rms_blocked(x, g):
    return jnp.concatenate(
        [_rms(x[rows], g).astype(BF16) for rows in _row_blocks(x.shape[0], NORM_BLOCKS)], axis=0)


def _swiglu(h, wg_ref, wu_ref, wd_ref):
    y = None
    chunks = list(zip(FF_BOUNDS[:-1], FF_BOUNDS[1:]))
    for c0, c1 in chunks:
        gate = jnp.dot(h, wg_ref[:, c0:c1], preferred_element_type=F32)
        up = jnp.dot(h, wu_ref[:, c0:c1], preferred_element_type=F32)
        act = (gate * jax.nn.sigmoid(gate) * up).astype(BF16)
        if (c0, c1) != chunks[-1]:
            part = jnp.dot(act, wd_ref[c0:c1, :], preferred_element_type=F32)
            y = part if y is None else y + part
    out = []
    for rows in _row_blocks(h.shape[0], TAIL_BLOCKS):
        part = jnp.dot(act[rows], wd_ref[c0:c1, :], preferred_element_type=F32)
        out.append((rows, part if y is None else y[rows] + part))
    return out


def _segment_cumsum(x, seg):
    row = lax.broadcasted_iota(jnp.int32, x.shape, 0) & (seg - 1)
    shift = 1
    while shift < seg:
        x = x + jnp.where(row >= shift, pltpu.roll(x, shift, 0), 0.0)
        shift *= 2
    return x


def _split3(x):
    hi = x.astype(BF16)
    r1 = x - hi.astype(F32)
    mid = r1.astype(BF16)
    lo = (r1 - mid.astype(F32)).astype(BF16)
    return hi, mid, lo


def _const_spec(shape):
    n = len(shape)
    return pl.BlockSpec(shape, lambda *_: (0,) * n, pipeline_mode=pl.Buffered(1))


def _ffn1_kernel(x_ref, g_ref, wg_ref, wu_ref, wd_ref, *rest):
    n_cast = (len(rest) - 1) // 2
    o_ref = rest[n_cast]
    for src, dst in zip(rest[:n_cast], rest[n_cast + 1:]):
        dst[...] = src[...].astype(BF16)
    h = _rms_blocked(x_ref[...], g_ref[...])
    for rows, y in _swiglu(h, wg_ref, wu_ref, wd_ref):
        o_ref[rows, :] = x_ref[rows, :] + 0.5 * y


def _ffn1(x, g, wg, wu, wd, later_weights):
    n = x.shape[0]
    steps = n // TM_FFN
    row = pl.BlockSpec((TM_FFN, D_MODEL), lambda i: (i, 0))

    def cast_spec(w):
        hold = next(k for k in (1, 2, 4, 8) if (w.shape[0] * k) % (steps * BF16_SUBLANES) == 0)
        return pl.BlockSpec((w.shape[0] * hold // steps, w.shape[1]), lambda i: (i // hold, 0))

    casts = [cast_spec(w) for w in later_weights]
    outs = pl.pallas_call(
        _ffn1_kernel,
        out_shape=[jax.ShapeDtypeStruct((n, D_MODEL), F32)]
        + [jax.ShapeDtypeStruct(w.shape, BF16) for w in later_weights],
        grid=(steps,),
        in_specs=[row, _const_spec((1, D_MODEL)), _const_spec((D_MODEL, D_FF)),
                  _const_spec((D_MODEL, D_FF)), _const_spec((D_FF, D_MODEL))] + casts,
        out_specs=[row] + casts,
        compiler_params=pltpu.CompilerParams(
            dimension_semantics=("arbitrary",), vmem_limit_bytes=VMEM_LIMIT),
        name="ffn1",
    )(x, g, wg, wu, wd, *later_weights)
    return outs[0], outs[1:]


_DECAY_BLK = 256
_N_TERMS = 3
_BIAS_LANES = _N_TERMS * FOX_HEADS
_FF_LANES = 2 * _BIAS_LANES
_LOG2E = 1.4426950408889634
_NT = (((1,), (1,)), ((), ()))
_TN = (((0,), (0,)), ((), ()))
_N_CHUNK = TM_PROJ // CHUNK
_FOX_PIECE = 256
_EARLY_FILLERS = 2
_DK_SHIFT = GLA_DK.bit_length() - 1
_DV_SHIFT = GLA_DV.bit_length() - 1


def _decay_operands(small, bias, offset):
    lane = lax.broadcasted_iota(jnp.int32, (_DECAY_BLK, SMALL_W), 1)
    term = (lane >> 3) % _N_TERMS
    lane_all = lax.broadcasted_iota(jnp.int32, small.shape, 1)
    log_f = jnp.where(lane_all < _FF_LANES, _log_sigmoid(small + bias), 0.0)
    local = _segment_cumsum(log_f, _DECAY_BLK)
    ka, qa = [], []
    for b0 in range(0, small.shape[0], _DECAY_BLK):
        cum = local[b0:b0 + _DECAY_BLK] + offset
        offset = cum[_DECAY_BLK - 1:_DECAY_BLK, :]
        hi, mid, lo = _split3(cum * _LOG2E)
        f_term = jnp.where(term == 0, hi.astype(F32),
                           jnp.where(term == 1, mid.astype(F32), lo.astype(F32)))
        one = jnp.ones_like(f_term)
        zero = jnp.zeros_like(f_term)
        ka.append(jnp.where(lane < _BIAS_LANES, -f_term,
                            jnp.where(lane < _FF_LANES, one, zero)).astype(BF16))
        qa.append(jnp.where(lane < _BIAS_LANES, one,
                            jnp.where(lane < _FF_LANES, f_term, zero)).astype(BF16))
    return jnp.concatenate(ka, axis=0), jnp.concatenate(qa, axis=0), offset


def _gla_tile(gq, gk, gv, go, small, w2, b2, g_norm, state, o_ref, fillers):
    fillers = list(fillers)
    z = jnp.dot(small.astype(BF16), w2, preferred_element_type=F32) + b2
    for _ in range(min(_EARLY_FILLERS, len(fillers))):
        fillers.pop(0)()
    every = max(1, (2 * _N_CHUNK) // max(1, len(fillers)))

    def maybe_fill(step):
        if fillers and step % every == every - 1:
            fillers.pop(0)()

    log_a = _log_sigmoid(z) / GLA_TEMP

    g_cum_all = _segment_cumsum(log_a, CHUNK)
    chunks = [slice(n * CHUNK, (n + 1) * CHUNK) for n in range(_N_CHUNK)]
    bd_mask = ((lax.broadcasted_iota(jnp.int32, (GLA_KW, GLA_VW), 0) >> _DK_SHIFT)
               == (lax.broadcasted_iota(jnp.int32, (GLA_KW, GLA_VW), 1) >> _DV_SHIFT))

    g_tot_rows = jnp.concatenate(
        [g_cum_all[rows.stop - 1:rows.stop, :] for rows in chunks]
        + [jnp.zeros((LANES - _N_CHUNK, GLA_KW), F32)], axis=0)
    decay_cols = jnp.exp(g_tot_rows.T)

    deltas = []
    for n, rows in enumerate(chunks):
        maybe_fill(n)
        g_cum = g_cum_all[rows]
        k_dec = gk[rows] * jnp.exp(g_cum[CHUNK - 1:CHUNK, :] - g_cum)
        deltas.append(jnp.concatenate(
            [lax.dot_general(k_dec[:, h * GLA_DK:(h + 1) * GLA_DK].astype(BF16),
                             gv[rows, h * GLA_DV:(h + 1) * GLA_DV], _TN,
                             preferred_element_type=F32)
             for h in range(GLA_HEADS)], axis=0))

    states = []
    for n, delta in enumerate(deltas):
        state = decay_cols[:, n:n + 1] * state + delta
        states.append(state.astype(BF16))

    for n, (rows, st) in enumerate(zip(chunks, states)):
        maybe_fill(_N_CHUNK + n)
        st_bd = jnp.where(bd_mask, jnp.concatenate([st] * GLA_HEADS, axis=1), jnp.zeros((), BF16))
        o_all = jnp.dot(gq[rows], st_bd, preferred_element_type=F32)
        for h in range(GLA_HEADS):
            o = _rms(o_all[:, h * GLA_DV:(h + 1) * GLA_DV], g_norm)
            gate = go[rows, h * GLA_DV:(h + 1) * GLA_DV]
            o_ref[0, rows, h * GLA_DV:(h + 1) * GLA_DV] = (
                o * (gate * jax.nn.sigmoid(gate))).astype(o_ref.dtype)
    for fill in fillers:
        fill()
    return state


def _pack_w_in(wt_ref, w_ref):
    def put(c0, rows, scale=None):
        blk = wt_ref[rows, :]
        if scale is not None:
            blk = blk * scale
        w_ref[:, c0:c0 + blk.shape[0]] = blk.T.astype(BF16)

    def src(i):
        return slice(_IN_OFFS[i], _IN_OFFS[i + 1])

    put(_C_FQ, src(0), _LOG2E * FOX_DH ** -0.5)
    put(_C_FK, src(1))
    put(_C_FV, src(2))
    put(_C_GQ, src(4), GLA_DK ** -0.5)
    put(_C_GK, src(5))
    put(_C_GV, src(6))
    put(_C_GO, src(8))
    small = jnp.concatenate(
        [wt_ref[src(3), :]] * (_FF_LANES // FOX_HEADS) + [wt_ref[src(7), :]]
        + [jnp.zeros((SMALL_W - _FF_LANES - GLA_RANK, D_MODEL), F32)], axis=0)
    w_ref[:, _C_SM:] = small.T.astype(BF16)


def _mix_in_kernel(x_ref, g_ref, wt_ref, fb_ref, w2_ref, b2_ref, gn_ref,
                   fq_ref, fk_ref, fv_ref, ka_ref, qa_ref, gla_ref, st_ref, off_ref, w_ref):
    @pl.when((pl.program_id(0) == 0) & (pl.program_id(1) == 0))
    def _():
        _pack_w_in(wt_ref, w_ref)

    @pl.when(pl.program_id(1) == 0)
    def _():
        st_ref[...] = jnp.zeros_like(st_ref)
        off_ref[...] = jnp.zeros_like(off_ref)

    h = _rms_blocked(x_ref[0], g_ref[...])

    def proj(c0, width):
        return jnp.dot(h, w_ref[:, c0:c0 + width], preferred_element_type=F32)

    small = proj(_C_SM, SMALL_W)
    gk = proj(_C_GK, GLA_KW)
    gv = proj(_C_GV, GLA_VW).astype(BF16)
    gq = proj(_C_GQ, GLA_KW).astype(BF16)
    go = proj(_C_GO, GLA_VW)

    def fox_proj(out_ref, c0, col):
        def run():
            out_ref[0, :, col:col + _FOX_PIECE] = proj(c0 + col, _FOX_PIECE).astype(BF16)
        return run

    fillers = [fox_proj(ref, c0, col)
               for ref, c0 in ((fq_ref, _C_FQ), (fk_ref, _C_FK), (fv_ref, _C_FV))
               for col in range(0, FOX_WIDTH, _FOX_PIECE)]
    st_ref[...] = _gla_tile(gq, gk, gv, go, small, w2_ref[...], b2_ref[...], gn_ref[...],
                            st_ref[...], gla_ref, fillers)
    ka_ref[0], qa_ref[0], off_ref[...] = _decay_operands(small, fb_ref[...], off_ref[...])


def _mix_in(x1, g, w_t, fox_bias, w2, b2, gn):
    b, s, _ = x1.shape
    assert w_t.shape == (_IN_OFFS[-1], D_MODEL)

    def rows(width):
        return pl.BlockSpec((1, TM_PROJ, width), lambda i, t: (i, t, 0))

    widths = (FOX_WIDTH, FOX_WIDTH, FOX_WIDTH, SMALL_W, SMALL_W, GLA_VW)
    return pl.pallas_call(
        _mix_in_kernel,
        out_shape=[jax.ShapeDtypeStruct((b, s, wd), BF16) for wd in widths],
        grid=(b, s // TM_PROJ),
        in_specs=[rows(D_MODEL), _const_spec((1, D_MODEL)), _const_spec(w_t.shape),
                  _const_spec((1, SMALL_W)), _const_spec((SMALL_W, GLA_KW)),
                  _const_spec((1, GLA_KW)), _const_spec((1, GLA_DV))],
        out_specs=[rows(wd) for wd in widths],
        scratch_shapes=[pltpu.VMEM((GLA_KW, GLA_DV), F32), pltpu.VMEM((1, SMALL_W), F32),
                        pltpu.VMEM((D_MODEL, PROJ_W), BF16)],
        compiler_params=pltpu.CompilerParams(
            dimension_semantics=("arbitrary", "arbitrary"), vmem_limit_bytes=VMEM_LIMIT),
        name="mix_in",
    )(x1, g, w_t, fox_bias, w2, b2, gn)


_LOOKAHEAD = 2
PAIRS_PER_STEP = 2

def _attn_kernel(q_ref, k_ref, v_ref, qa_ref, ka_ref, o_ref):
    seq = q_ref.shape[1]
    lane = lax.broadcasted_iota(jnp.int32, (TQ, LANES), 1)
    k_idx = lax.broadcasted_iota(jnp.int32, (TQ, TQ), 0)
    q_idx = lax.broadcasted_iota(jnp.int32, (TQ, TQ), 1)
    causal = k_idx <= q_idx
    pair_lanes = [slice(p * LANES, (p + 1) * LANES) for p in range(PAIRS_PER_STEP)]

    k_aug = [jnp.concatenate([k_ref[0, :, cols], ka_ref[0]], axis=1) for cols in pair_lanes]
    v_t = [v_ref[0, :, cols].T for cols in pair_lanes]

    def logits(p, qi):
        q0 = qi * TQ
        klen = q0 + TQ
        q = q_ref[0, q0:klen, pair_lanes[p]]
        qa = qa_ref[0, q0:klen, :]
        head0 = 2 * (pl.program_id(1) * PAIRS_PER_STEP + p)
        out = []
        for hh in range(2):
            in_head = (lane >= hh * FOX_DH) & (lane < (hh + 1) * FOX_DH)
            bias_lane = ((lane & (FOX_HEADS - 1)) == head0 + hh) & (lane < _FF_LANES)
            q_aug = jnp.concatenate([jnp.where(in_head, q, jnp.zeros_like(q)),
                                     jnp.where(bias_lane, qa, jnp.zeros_like(qa))], axis=1)
            out.append(lax.dot_general(k_aug[p][:klen], q_aug, _NT, preferred_element_type=F32))
        return out

    def finish(p, qi, s_both):
        q0 = qi * TQ
        klen = q0 + TQ
        outs = []
        for hh in range(2):
            s_t = s_both[hh]
            s_diag = jnp.where(causal, s_t[q0:], -jnp.inf)
            m = jnp.max(s_diag, axis=0, keepdims=True)
            if qi > 0:
                m = jnp.maximum(m, jnp.max(s_t[:q0], axis=0, keepdims=True))
            o_t, l = None, None
            for k0 in range(0, klen, TQ):
                p_t = jnp.exp2((s_diag if k0 == q0 else s_t[k0:k0 + TQ]) - m)
                l_t = jnp.sum(p_t, axis=0, keepdims=True)
                part = jnp.dot(v_t[p][hh * FOX_DH:(hh + 1) * FOX_DH, k0:k0 + TQ],
                               p_t.astype(BF16), preferred_element_type=F32)
                o_t = part if o_t is None else o_t + part
                l = l_t if l is None else l + l_t
            outs.append(o_t * (1.0 / l))
        o_ref[0, q0:klen, pair_lanes[p]] = jnp.concatenate(outs, axis=0).T.astype(o_ref.dtype)

    n_blocks = seq // TQ
    order = list(range(1, n_blocks, 2)) + list(range(n_blocks - 2 + n_blocks % 2, -1, -2))
    items = [(p, qi) for p in range(PAIRS_PER_STEP) for qi in order]
    ready = {item: logits(*item) for item in items[:_LOOKAHEAD]}
    for pos, item in enumerate(items):
        if pos + _LOOKAHEAD < len(items):
            ahead = items[pos + _LOOKAHEAD]
            ready[ahead] = logits(*ahead)
        finish(*item, ready.pop(item))


def _fox_attn(fq, fk, fv, qa, ka):
    b, s, _ = fq.shape
    qkv = pl.BlockSpec((1, s, PAIRS_PER_STEP * LANES), lambda i, h: (i, 0, h))
    aug = pl.BlockSpec((1, s, SMALL_W), lambda i, h: (i, 0, 0))
    return pl.pallas_call(
        _attn_kernel,
        out_shape=jax.ShapeDtypeStruct((b, s, FOX_WIDTH), BF16),
        grid=(b, FOX_HEADS // (2 * PAIRS_PER_STEP)),
        in_specs=[qkv, qkv, qkv, aug, aug],
        out_specs=qkv,
        compiler_params=pltpu.CompilerParams(
            dimension_semantics=("arbitrary", "arbitrary"), vmem_limit_bytes=VMEM_LIMIT),
        name="fox_attn",
    )(fq, fk, fv, qa, ka)


def _out_kernel(x_ref, fox_ref, gla_ref, wo_ref, g_ref, wg_ref, wu_ref, wd_ref, gf_ref, o_ref):
    x2, h = [], []
    for rows in _row_blocks(x_ref.shape[0], NORM_BLOCKS):
        mix = (jnp.dot(fox_ref[rows, :], wo_ref[:FOX_WIDTH, :], preferred_element_type=F32)
               + jnp.dot(gla_ref[rows, :], wo_ref[FOX_WIDTH:, :], preferred_element_type=F32))
        x2.append(x_ref[rows, :] + mix)
        h.append(_rms(x2[-1], g_ref[...]).astype(BF16))
    x2 = jnp.concatenate(x2, axis=0)
    for rows, y in _swiglu(jnp.concatenate(h, axis=0), wg_ref, wu_ref, wd_ref):
        o_ref[rows, :] = _rms(x2[rows] + 0.5 * y, gf_ref[...])


def _out_ffn2(x1, fox, gla, wo, g, wg, wu, wd, gf):
    n = x1.shape[0]

    def rows(width):
        return pl.BlockSpec((TM_FFN, width), lambda i: (i, 0))

    return pl.pallas_call(
        _out_kernel,
        out_shape=jax.ShapeDtypeStruct((n, D_MODEL), F32),
        grid=(n // TM_FFN,),
        in_specs=[rows(D_MODEL), rows(FOX_WIDTH), rows(GLA_VW),
                  _const_spec((FOX_WIDTH + GLA_VW, D_MODEL)), _const_spec((1, D_MODEL)),
                  _const_spec((D_MODEL, D_FF)), _const_spec((D_MODEL, D_FF)),
                  _const_spec((D_FF, D_MODEL)), _const_spec((1, D_MODEL))],
        out_specs=rows(D_MODEL),
        compiler_params=pltpu.CompilerParams(
            dimension_semantics=("arbitrary",), vmem_limit_bytes=VMEM_LIMIT),
        name="out_ffn2",
    )(x1, fox, gla, wo, g, wg, wu, wd, gf)


def kernel(x, ffn1_norm, ffn1_w_gate, ffn1_w_up, ffn1_w_down, mix_norm, w_in, fox_forget_bias,
           gla_w_gate_up, gla_gate_bias, gla_out_norm, w_out, ffn2_norm, ffn2_w_gate, ffn2_w_up,
           ffn2_w_down, final_norm):
    b, s, d = x.shape
    n = b * s
    assert ffn1_norm.shape[0] == 1, "the final norm is fused into the single layer's last call"
    x = x.reshape(n, d)
    for l in range(1):
        x1, (wo, wg2, wu2, wd2) = _ffn1(
            x, ffn1_norm[l].reshape(1, d), ffn1_w_gate[l].astype(BF16),
            ffn1_w_up[l].astype(BF16), ffn1_w_down[l].astype(BF16),
            [w_out[l], ffn2_w_gate[l], ffn2_w_up[l], ffn2_w_down[l]])

        bias = jnp.zeros((1, SMALL_W), F32).at[0, :_FF_LANES].set(
            jnp.tile(fox_forget_bias[l], _FF_LANES // FOX_HEADS))
        w2 = jnp.zeros((SMALL_W, GLA_KW), F32).at[_FF_LANES:_FF_LANES + GLA_RANK].set(
            gla_w_gate_up[l]).astype(BF16)
        fq, fk, fv, ka, qa, gla = _mix_in(
            x1.reshape(b, s, d), mix_norm[l].reshape(1, d), w_in[l].T, bias, w2,
            gla_gate_bias[l].reshape(1, GLA_KW), gla_out_norm[l].reshape(1, GLA_DV))
        fox = _fox_attn(fq, fk, fv, qa, ka)

        x = _out_ffn2(x1, fox.reshape(n, -1), gla.reshape(n, -1), wo,
                      ffn2_norm[l].reshape(1, d), wg2, wu2, wd2,
                      final_norm.reshape(1, d))
    return x.reshape(b, s, d)
```

```python
import jax
import jax.numpy as jnp
from jax import lax
from jax.experimental import pallas as pl
from jax.experimental.pallas import tpu as pltpu

F32 = jnp.float32
BF16 = jnp.bfloat16

D_MODEL = 1024
D_FF = 2816
EPS = 1e-6
CHUNK = 64

FOX_HEADS = 8
FOX_DH = 64
FOX_WIDTH = FOX_HEADS * FOX_DH
GLA_HEADS = 4
GLA_DK = 64
GLA_DV = 128
GLA_KW = GLA_HEADS * GLA_DK
GLA_VW = GLA_HEADS * GLA_DV
GLA_RANK = 16
GLA_TEMP = 16.0

LANES = 128
BF16_SUBLANES = 16
SMALL_W = LANES
_C_FQ = 0
_C_FK = _C_FQ + FOX_WIDTH
_C_FV = _C_FK + FOX_WIDTH
_C_GQ = _C_FV + FOX_WIDTH
_C_GK = _C_GQ + GLA_KW
_C_GV = _C_GK + GLA_KW
_C_GO = _C_GV + GLA_VW
_C_SM = _C_GO + GLA_VW
PROJ_W = _C_SM + SMALL_W
_IN_OFFS = (0,)
for _w in (FOX_WIDTH, FOX_WIDTH, FOX_WIDTH, FOX_HEADS, GLA_KW, GLA_KW, GLA_VW, GLA_RANK, GLA_VW):
    _IN_OFFS += (_IN_OFFS[-1] + _w,)

TM_FFN = 1024
FF_BOUNDS = (0, 1024, 2048, D_FF)
NORM_BLOCKS = 4
TAIL_BLOCKS = 2
TM_PROJ = 1024
TQ = 256
VMEM_LIMIT = 56 * 1024 * 1024


def _rms(x, g):
    return x * lax.rsqrt(jnp.mean(x * x, axis=-1, keepdims=True) + EPS) * g


def _log_sigmoid(z):
    return jnp.minimum(z, 0.0) - jnp.log(1.0 + jnp.exp(-jnp.abs(z)))


def _row_blocks(n_rows, n_blocks):
    step = n_rows // n_blocks
    return [slice(r0, r0 + step) for r0 in range(0, n_rows, step)]


def _rms_blocked(x, g):
    return jnp.concatenate(
        [_rms(x[rows], g).astype(BF16) for rows in _row_blocks(x.shape[0], NORM_BLOCKS)], axis=0)


def _swiglu(h, wg_ref, wu_ref, wd_ref):
    y = None
    chunks = list(zip(FF_BOUNDS[:-1], FF_BOUNDS[1:]))
    for c0, c1 in chunks:
        gate = jnp.dot(h, wg_ref[:, c0:c1], preferred_element_type=F32)
        up = jnp.dot(h, wu_ref[:, c0:c1], preferred_element_type=F32)
        act = (gate * jax.nn.sigmoid(gate) * up).astype(BF16)
        if (c0, c1) != chunks[-1]:
            part = jnp.dot(act, wd_ref[c0:c1, :], preferred_element_type=F32)
            y = part if y is None else y + part
    out = []
    for rows in _row_blocks(h.shape[0], TAIL_BLOCKS):
        part = jnp.dot(act[rows], wd_ref[c0:c1, :], preferred_element_type=F32)
        out.append((rows, part if y is None else y[rows] + part))
    return out


def _segment_cumsum(x, seg):
    row = lax.broadcasted_iota(jnp.int32, x.shape, 0) & (seg - 1)
    shift = 1
    while shift < seg:
        x = x + jnp.where(row >= shift, pltpu.roll(x, shift, 0), 0.0)
        shift *= 2
    return x


def _split3(x):
    hi = x.astype(BF16)
    r1 = x - hi.astype(F32)
    mid = r1.astype(BF16)
    lo = (r1 - mid.astype(F32)).astype(BF16)
    return hi, mid, lo


def _const_spec(shape):
    n = len(shape)
    return pl.BlockSpec(shape, lambda *_: (0,) * n, pipeline_mode=pl.Buffered(1))


def _ffn1_kernel(x_ref, g_ref, wg_ref, wu_ref, wd_ref, *rest):
    n_cast = (len(rest) - 1) // 2
    o_ref = rest[n_cast]
    for src, dst in zip(rest[:n_cast], rest[n_cast + 1:]):
        dst[...] = src[...].astype(BF16)
    h = _rms_blocked(x_ref[...], g_ref[...])
    for rows, y in _swiglu(h, wg_ref, wu_ref, wd_ref):
        o_ref[rows, :] = x_ref[rows, :] + 0.5 * y


def _ffn1(x, g, wg, wu, wd, later_weights):
    n = x.shape[0]
    steps = n // TM_FFN
    row = pl.BlockSpec((TM_FFN, D_MODEL), lambda i: (i, 0))

    def cast_spec(w):
        hold = next(k for k in (1, 2, 4, 8) if (w.shape[0] * k) % (steps * BF16_SUBLANES) == 0)
        return pl.BlockSpec((w.shape[0] * hold // steps, w.shape[1]), lambda i: (i // hold, 0))

    casts = [cast_spec(w) for w in later_weights]
    outs = pl.pallas_call(
        _ffn1_kernel,
        out_shape=[jax.ShapeDtypeStruct((n, D_MODEL), F32)]
        + [jax.ShapeDtypeStruct(w.shape, BF16) for w in later_weights],
        grid=(steps,),
        in_specs=[row, _const_spec((1, D_MODEL)), _const_spec((D_MODEL, D_FF)),
                  _const_spec((D_MODEL, D_FF)), _const_spec((D_FF, D_MODEL))] + casts,
        out_specs=[row] + casts,
        compiler_params=pltpu.CompilerParams(
            dimension_semantics=("arbitrary",), vmem_limit_bytes=VMEM_LIMIT),
        name="ffn1",
    )(x, g, wg, wu, wd, *later_weights)
    return outs[0], outs[1:]


_DECAY_BLK = 256
_N_TERMS = 3
_BIAS_LANES = _N_TERMS * FOX_HEADS
_FF_LANES = 2 * _BIAS_LANES
_LOG2E = 1.4426950408889634
_NT = (((1,), (1,)), ((), ()))
_TN = (((0,), (0,)), ((), ()))
_N_CHUNK = TM_PROJ // CHUNK
_FOX_PIECE = 256
_EARLY_FILLERS = 3
_LATE_FILLERS = 1
_DK_SHIFT = GLA_DK.bit_length() - 1
_DV_SHIFT = GLA_DV.bit_length() - 1


def _decay_operands(small, bias, offset):
    lane = lax.broadcasted_iota(jnp.int32, (_DECAY_BLK, SMALL_W), 1)
    term = (lane >> 3) % _N_TERMS
    lane_all = lax.broadcasted_iota(jnp.int32, small.shape, 1)
    log_f = jnp.where(lane_all < _FF_LANES, _log_sigmoid(small + bias), 0.0)
    local = _segment_cumsum(log_f, _DECAY_BLK)
    ka, qa = [], []
    for b0 in range(0, small.shape[0], _DECAY_BLK):
        cum = local[b0:b0 + _DECAY_BLK] + offset
        offset = cum[_DECAY_BLK - 1:_DECAY_BLK, :]
        hi, mid, lo = _split3(cum * _LOG2E)
        f_term = jnp.where(term == 0, hi.astype(F32),
                           jnp.where(term == 1, mid.astype(F32), lo.astype(F32)))
        one = jnp.ones_like(f_term)
        zero = jnp.zeros_like(f_term)
        ka.append(jnp.where(lane < _BIAS_LANES, -f_term,
                            jnp.where(lane < _FF_LANES, one, zero)).astype(BF16))
        qa.append(jnp.where(lane < _BIAS_LANES, one,
                            jnp.where(lane < _FF_LANES, f_term, zero)).astype(BF16))
    return jnp.concatenate(ka, axis=0), jnp.concatenate(qa, axis=0), offset


def _gla_tile(gq, gk, gv, go, small, w2, b2, g_norm, state, o_ref, fillers):
    fillers = list(fillers)
    z = jnp.dot(small.astype(BF16), w2, preferred_element_type=F32) + b2
    for _ in range(min(_EARLY_FILLERS, len(fillers))):
        fillers.pop(0)()
    every = max(1, (2 * _N_CHUNK) // max(1, len(fillers)))

    def maybe_fill(step):
        if fillers and step % every == every - 1:
            fillers.pop(0)()

    log_a = _log_sigmoid(z) / GLA_TEMP

    g_cum_all = _segment_cumsum(log_a, CHUNK)
    chunks = [slice(n * CHUNK, (n + 1) * CHUNK) for n in range(_N_CHUNK)]
    bd_mask = ((lax.broadcasted_iota(jnp.int32, (GLA_KW, GLA_VW), 0) >> _DK_SHIFT)
               == (lax.broadcasted_iota(jnp.int32, (GLA_KW, GLA_VW), 1) >> _DV_SHIFT))

    g_tot_rows = jnp.concatenate(
        [g_cum_all[rows.stop - 1:rows.stop, :] for rows in chunks]
        + [jnp.zeros((LANES - _N_CHUNK, GLA_KW), F32)], axis=0)
    decay_cols = jnp.exp(g_tot_rows.T)

    deltas = []
    for n, rows in enumerate(chunks):
        maybe_fill(n)
        g_cum = g_cum_all[rows]
        k_dec = gk[rows] * jnp.exp(g_cum[CHUNK - 1:CHUNK, :] - g_cum)
        deltas.append(jnp.concatenate(
            [lax.dot_general(k_dec[:, h * GLA_DK:(h + 1) * GLA_DK].astype(BF16),
                             gv[rows, h * GLA_DV:(h + 1) * GLA_DV], _TN,
                             preferred_element_type=F32)
             for h in range(GLA_HEADS)], axis=0))

    states = []
    for n, delta in enumerate(deltas):
        state = decay_cols[:, n:n + 1] * state + delta
        states.append(state.astype(BF16))

    for n, (rows, st) in enumerate(zip(chunks, states)):
        maybe_fill(_N_CHUNK + n)
        st_bd = jnp.where(bd_mask, jnp.concatenate([st] * GLA_HEADS, axis=1), jnp.zeros((), BF16))
        o_all = jnp.dot(gq[rows], st_bd, preferred_element_type=F32)
        for h in range(GLA_HEADS):
            o = _rms(o_all[:, h * GLA_DV:(h + 1) * GLA_DV], g_norm)
            gate = go[rows, h * GLA_DV:(h + 1) * GLA_DV]
            o_ref[0, rows, h * GLA_DV:(h + 1) * GLA_DV] = (
                o * (gate * jax.nn.sigmoid(gate))).astype(o_ref.dtype)
    for fill in fillers:
        fill()
    return state


def _pack_w_in(wt_ref, w_ref):
    def put(c0, rows, scale=None):
        blk = wt_ref[rows, :]
        if scale is not None:
            blk = blk * scale
        w_ref[:, c0:c0 + blk.shape[0]] = blk.T.astype(BF16)

    def src(i):
        return slice(_IN_OFFS[i], _IN_OFFS[i + 1])

    put(_C_FQ, src(0), _LOG2E * FOX_DH ** -0.5)
    put(_C_FK, src(1))
    put(_C_FV, src(2))
    put(_C_GQ, src(4), GLA_DK ** -0.5)
    put(_C_GK, src(5))
    put(_C_GV, src(6))
    put(_C_GO, src(8))
    small = jnp.concatenate(
        [wt_ref[src(3), :]] * (_FF_LANES // FOX_HEADS) + [wt_ref[src(7), :]]
        + [jnp.zeros((SMALL_W - _FF_LANES - GLA_RANK, D_MODEL), F32)], axis=0)
    w_ref[:, _C_SM:] = small.T.astype(BF16)


def _mix_in_kernel(x_ref, g_ref, wt_ref, fb_ref, w2_ref, b2_ref, gn_ref,
                   fq_ref, fk_ref, fv_ref, ka_ref, qa_ref, gla_ref, st_ref, off_ref, w_ref):
    @pl.when((pl.program_id(0) == 0) & (pl.program_id(1) == 0))
    def _():
        _pack_w_in(wt_ref, w_ref)

    @pl.when(pl.program_id(1) == 0)
    def _():
        st_ref[...] = jnp.zeros_like(st_ref)
        off_ref[...] = jnp.zeros_like(off_ref)

    h = _rms_blocked(x_ref[0], g_ref[...])

    def proj(c0, width):
        return jnp.dot(h, w_ref[:, c0:c0 + width], preferred_element_type=F32)

    small = proj(_C_SM, SMALL_W)
    gk = proj(_C_GK, GLA_KW)
    gv = proj(_C_GV, GLA_VW).astype(BF16)
    gq = proj(_C_GQ, GLA_KW).astype(BF16)
    go = proj(_C_GO, GLA_VW)

    def fox_proj(out_ref, c0, col):
        def run():
            out_ref[0, :, col:col + _FOX_PIECE] = proj(c0 + col, _FOX_PIECE).astype(BF16)
        return run

    fillers = [fox_proj(ref, c0, col)
               for ref, c0 in ((fq_ref, _C_FQ), (fk_ref, _C_FK), (fv_ref, _C_FV))
               for col in range(0, FOX_WIDTH, _FOX_PIECE)]
    st_ref[...] = _gla_tile(gq, gk, gv, go, small, w2_ref[...], b2_ref[...], gn_ref[...],
                            st_ref[...], gla_ref, fillers[:-_LATE_FILLERS])
    for fill in fillers[-_LATE_FILLERS:]:
        fill()
    ka_ref[0], qa_ref[0], off_ref[...] = _decay_operands(small, fb_ref[...], off_ref[...])


def _mix_in(x1, g, w_t, fox_bias, w2, b2, gn):
    b, s, _ = x1.shape
    assert w_t.shape == (_IN_OFFS[-1], D_MODEL)

    def rows(width):
        return pl.BlockSpec((1, TM_PROJ, width), lambda i, t: (i, t, 0))

    widths = (FOX_WIDTH, FOX_WIDTH, FOX_WIDTH, SMALL_W, SMALL_W, GLA_VW)
    return pl.pallas_call(
        _mix_in_kernel,
        out_shape=[jax.ShapeDtypeStruct((b, s, wd), BF16) for wd in widths],
        grid=(b, s // TM_PROJ),
        in_specs=[rows(D_MODEL), _const_spec((1, D_MODEL)), _const_spec(w_t.shape),
                  _const_spec((1, SMALL_W)), _const_spec((SMALL_W, GLA_KW)),
                  _const_spec((1, GLA_KW)), _const_spec((1, GLA_DV))],
        out_specs=[rows(wd) for wd in widths],
        scratch_shapes=[pltpu.VMEM((GLA_KW, GLA_DV), F32), pltpu.VMEM((1, SMALL_W), F32),
                        pltpu.VMEM((D_MODEL, PROJ_W), BF16)],
        compiler_params=pltpu.CompilerParams(
            dimension_semantics=("arbitrary", "arbitrary"), vmem_limit_bytes=VMEM_LIMIT),
        name="mix_in",
    )(x1, g, w_t, fox_bias, w2, b2, gn)


_LOOKAHEAD = 2
PAIRS_PER_STEP = 2

def _attn_kernel(q_ref, k_ref, v_ref, qa_ref, ka_ref, o_ref):
    seq = q_ref.shape[1]
    lane = lax.broadcasted_iota(jnp.int32, (TQ, LANES), 1)
    k_idx = lax.broadcasted_iota(jnp.int32, (TQ, TQ), 0)
    q_idx = lax.broadcasted_iota(jnp.int32, (TQ, TQ), 1)
    causal = k_idx <= q_idx
    pair_lanes = [slice(p * LANES, (p + 1) * LANES) for p in range(PAIRS_PER_STEP)]

    k_aug = [jnp.concatenate([k_ref[0, :, cols], ka_ref[0]], axis=1) for cols in pair_lanes]
    v_t = [v_ref[0, :, cols].T for cols in pair_lanes]

    def logits(p, qi):
        q0 = qi * TQ
        klen = q0 + TQ
        q = q_ref[0, q0:klen, pair_lanes[p]]
        qa = qa_ref[0, q0:klen, :]
        head0 = 2 * (pl.program_id(1) * PAIRS_PER_STEP + p)
        out = []
        for hh in range(2):
            in_head = (lane >= hh * FOX_DH) & (lane < (hh + 1) * FOX_DH)
            bias_lane = ((lane & (FOX_HEADS - 1)) == head0 + hh) & (lane < _FF_LANES)
            q_aug = jnp.concatenate([jnp.where(in_head, q, jnp.zeros_like(q)),
                                     jnp.where(bias_lane, qa, jnp.zeros_like(qa))], axis=1)
            out.append(lax.dot_general(k_aug[p][:klen], q_aug, _NT, preferred_element_type=F32))
        return out

    def finish(p, qi, s_both):
        q0 = qi * TQ
        klen = q0 + TQ
        outs = []
        for hh in range(2):
            s_t = s_both[hh]
            s_diag = jnp.where(causal, s_t[q0:], -jnp.inf)
            m = jnp.max(s_diag, axis=0, keepdims=True)
            if qi > 0:
                m = jnp.maximum(m, jnp.max(s_t[:q0], axis=0, keepdims=True))
            o_t, l = None, None
            for k0 in range(0, klen, TQ):
                p_t = jnp.exp2((s_diag if k0 == q0 else s_t[k0:k0 + TQ]) - m)
                l_t = jnp.sum(p_t, axis=0, keepdims=True)
                part = jnp.dot(v_t[p][hh * FOX_DH:(hh + 1) * FOX_DH, k0:k0 + TQ],
                               p_t.astype(BF16), preferred_element_type=F32)
                o_t = part if o_t is None else o_t + part
                l = l_t if l is None else l + l_t
            outs.append(o_t * (1.0 / l))
        o_ref[0, q0:klen, pair_lanes[p]] = jnp.concatenate(outs, axis=0).T.astype(o_ref.dtype)

    n_blocks = seq // TQ
    order = list(range(1, n_blocks, 2)) + list(range(n_blocks - 2 + n_blocks % 2, -1, -2))
    items = [(p, qi) for p in range(PAIRS_PER_STEP) for qi in order]
    ready = {item: logits(*item) for item in items[:_LOOKAHEAD]}
    for pos, item in enumerate(items):
        if pos + _LOOKAHEAD < len(items):
            ahead = items[pos + _LOOKAHEAD]
            ready[ahead] = logits(*ahead)
        finish(*item, ready.pop(item))


def _fox_attn(fq, fk, fv, qa, ka):
    b, s, _ = fq.shape
    qkv = pl.BlockSpec((1, s, PAIRS_PER_STEP * LANES), lambda i, h: (i, 0, h))
    aug = pl.BlockSpec((1, s, SMALL_W), lambda i, h: (i, 0, 0))
    return pl.pallas_call(
        _attn_kernel,
        out_shape=jax.ShapeDtypeStruct((b, s, FOX_WIDTH), BF16),
        grid=(b, FOX_HEADS // (2 * PAIRS_PER_STEP)),
        in_specs=[qkv, qkv, qkv, aug, aug],
        out_specs=qkv,
        compiler_params=pltpu.CompilerParams(
            dimension_semantics=("arbitrary", "arbitrary"), vmem_limit_bytes=VMEM_LIMIT),
        name="fox_attn",
    )(fq, fk, fv, qa, ka)


def _out_kernel(x_ref, fox_ref, gla_ref, wo_ref, g_ref, wg_ref, wu_ref, wd_ref, gf_ref, o_ref):
    x2, h = [], []
    for rows in _row_blocks(x_ref.shape[0], NORM_BLOCKS):
        mix = (jnp.dot(fox_ref[rows, :], wo_ref[:FOX_WIDTH, :], preferred_element_type=F32)
               + jnp.dot(gla_ref[rows, :], wo_ref[FOX_WIDTH:, :], preferred_element_type=F32))
        x2.append(x_ref[rows, :] + mix)
        h.append(_rms(x2[-1], g_ref[...]).astype(BF16))
    x2 = jnp.concatenate(x2, axis=0)
    for rows, y in _swiglu(jnp.concatenate(h, axis=0), wg_ref, wu_ref, wd_ref):
        o_ref[rows, :] = _rms(x2[rows] + 0.5 * y, gf_ref[...])


def _out_ffn2(x1, fox, gla, wo, g, wg, wu, wd, gf):
    n = x1.shape[0]

    def rows(width):
        return pl.BlockSpec((TM_FFN, width), lambda i: (i, 0))

    return pl.pallas_call(
        _out_kernel,
        out_shape=jax.ShapeDtypeStruct((n, D_MODEL), F32),
        grid=(n // TM_FFN,),
        in_specs=[rows(D_MODEL), rows(FOX_WIDTH), rows(GLA_VW),
                  _const_spec((FOX_WIDTH + GLA_VW, D_MODEL)), _const_spec((1, D_MODEL)),
                  _const_spec((D_MODEL, D_FF)), _const_spec((D_MODEL, D_FF)),
                  _const_spec((D_FF, D_MODEL)), _const_spec((1, D_MODEL))],
        out_specs=rows(D_MODEL),
        compiler_params=pltpu.CompilerParams(
            dimension_semantics=("arbitrary",), vmem_limit_bytes=VMEM_LIMIT),
        name="out_ffn2",
    )(x1, fox, gla, wo, g, wg, wu, wd, gf)


def kernel(x, ffn1_norm, ffn1_w_gate, ffn1_w_up, ffn1_w_down, mix_norm, w_in, fox_forget_bias,
           gla_w_gate_up, gla_gate_bias, gla_out_norm, w_out, ffn2_norm, ffn2_w_gate, ffn2_w_up,
           ffn2_w_down, final_norm):
    b, s, d = x.shape
    n = b * s
    assert ffn1_norm.shape[0] == 1, "the final norm is fused into the single layer's last call"
    x = x.reshape(n, d)
    for l in range(1):
        x1, (wo, wg2, wu2, wd2) = _ffn1(
            x, ffn1_norm[l].reshape(1, d), ffn1_w_gate[l].astype(BF16),
            ffn1_w_up[l].astype(BF16), ffn1_w_down[l].astype(BF16),
            [w_out[l], ffn2_w_gate[l], ffn2_w_up[l], ffn2_w_down[l]])

        bias = jnp.zeros((1, SMALL_W), F32).at[0, :_FF_LANES].set(
            jnp.tile(fox_forget_bias[l], _FF_LANES // FOX_HEADS))
        w2 = jnp.zeros((SMALL_W, GLA_KW), F32).at[_FF_LANES:_FF_LANES + GLA_RANK].set(
            gla_w_gate_up[l]).astype(BF16)
        fq, fk, fv, ka, qa, gla = _mix_in(
            x1.reshape(b, s, d), mix_norm[l].reshape(1, d), w_in[l].T, bias, w2,
            gla_gate_bias[l].reshape(1, GLA_KW), gla_out_norm[l].reshape(1, GLA_DV))
        fox = _fox_attn(fq, fk, fv, qa, ka)

        x = _out_ffn2(x1, fox.reshape(n, -1), gla.reshape(n, -1), wo,
                      ffn2_norm[l].reshape(1, d), wg2, wu2, wd2,
                      final_norm.reshape(1, d))
    return x.reshape(b, s, d)
```

```python
import jax
import jax.numpy as jnp
from jax import lax
from jax.experimental import pallas as pl
from jax.experimental.pallas import tpu as pltpu

F32 = jnp.float32
BF16 = jnp.bfloat16

D_MODEL = 1024
D_FF = 2816
EPS = 1e-6
CHUNK = 64

FOX_HEADS = 8
FOX_DH = 64
FOX_WIDTH = FOX_HEADS * FOX_DH
GLA_HEADS = 4
GLA_DK = 64
GLA_DV = 128
GLA_KW = GLA_HEADS * GLA_DK
GLA_VW = GLA_HEADS * GLA_DV
GLA_RANK = 16
GLA_TEMP = 16.0

LANES = 128
BF16_SUBLANES = 16
SMALL_W = LANES
_C_FQ = 0
_C_FK = _C_FQ + FOX_WIDTH
_C_FV = _C_FK + FOX_WIDTH
_C_GQ = _C_FV + FOX_WIDTH
_C_GK = _C_GQ + GLA_KW
_C_GV = _C_GK + GLA_KW
_C_GO = _C_GV + GLA_VW
_C_SM = _C_GO + GLA_VW
PROJ_W = _C_SM + SMALL_W
_IN_OFFS = (0,)
for _w in (FOX_WIDTH, FOX_WIDTH, FOX_WIDTH, FOX_HEADS, GLA_KW, GLA_KW, GLA_VW, GLA_RANK, GLA_VW):
    _IN_OFFS += (_IN_OFFS[-1] + _w,)

TM_FFN = 1024
FF_BOUNDS = (0, 1024, 2048, D_FF)
NORM_BLOCKS = 4
TAIL_BLOCKS = 2
TM_PROJ = 1024
TQ = 256
VMEM_LIMIT = 56 * 1024 * 1024


def _rms(x, g):
    return x * lax.rsqrt(jnp.mean(x * x, axis=-1, keepdims=True) + EPS) * g


def _log_sigmoid(z):
    return jnp.minimum(z, 0.0) - jnp.log(1.0 + jnp.exp(-jnp.abs(z)))


def _row_blocks(n_rows, n_blocks):
    step = n_rows // n_blocks
    return [slice(r0, r0 + step) for r0 in range(0, n_rows, step)]


def _rms_blocked(x, g):
    return jnp.concatenate(
        [_rms(x[rows], g).astype(BF16) for rows in _row_blocks(x.shape[0], NORM_BLOCKS)], axis=0)


def _swiglu(h, wg_ref, wu_ref, wd_ref):
    y = None
    chunks = list(zip(FF_BOUNDS[:-1], FF_BOUNDS[1:]))
    for c0, c1 in chunks:
        gate = jnp.dot(h, wg_ref[:, c0:c1], preferred_element_type=F32)
        up = jnp.dot(h, wu_ref[:, c0:c1], preferred_element_type=F32)
        act = (gate * jax.nn.sigmoid(gate) * up).astype(BF16)
        if (c0, c1) != chunks[-1]:
            part = jnp.dot(act, wd_ref[c0:c1, :], preferred_element_type=F32)
            y = part if y is None else y + part
    out = []
    for rows in _row_blocks(h.shape[0], TAIL_BLOCKS):
        part = jnp.dot(act[rows], wd_ref[c0:c1, :], preferred_element_type=F32)
        out.append((rows, part if y is None else y[rows] + part))
    return out


def _segment_cumsum(x, seg):
    row = lax.broadcasted_iota(jnp.int32, x.shape, 0) & (seg - 1)
    shift = 1
    while shift < seg:
        x = x + jnp.where(row >= shift, pltpu.roll(x, shift, 0), 0.0)
        shift *= 2
    return x


def _split3(x):
    hi = x.astype(BF16)
    r1 = x - hi.astype(F32)
    mid = r1.astype(BF16)
    lo = (r1 - mid.astype(F32)).astype(BF16)
    return hi, mid, lo


def _const_spec(shape):
    n = len(shape)
    return pl.BlockSpec(shape, lambda *_: (0,) * n, pipeline_mode=pl.Buffered(1))


def _ffn1_kernel(x_ref, g_ref, wg_ref, wu_ref, wd_ref, *rest):
    n_cast = (len(rest) - 1) // 2
    o_ref = rest[n_cast]
    for src, dst in zip(rest[:n_cast], rest[n_cast + 1:]):
        dst[...] = src[...].astype(BF16)
    h = _rms_blocked(x_ref[...], g_ref[...])
    for rows, y in _swiglu(h, wg_ref, wu_ref, wd_ref):
        o_ref[rows, :] = x_ref[rows, :] + 0.5 * y


def _ffn1(x, g, wg, wu, wd, later_weights):
    n = x.shape[0]
    steps = n // TM_FFN
    row = pl.BlockSpec((TM_FFN, D_MODEL), lambda i: (i, 0))

    def cast_spec(w):
        hold = next(k for k in (1, 2, 4, 8) if (w.shape[0] * k) % (steps * BF16_SUBLANES) == 0)
        return pl.BlockSpec((w.shape[0] * hold // steps, w.shape[1]), lambda i: (i // hold, 0))

    casts = [cast_spec(w) for w in later_weights]
    outs = pl.pallas_call(
        _ffn1_kernel,
        out_shape=[jax.ShapeDtypeStruct((n, D_MODEL), F32)]
        + [jax.ShapeDtypeStruct(w.shape, BF16) for w in later_weights],
        grid=(steps,),
        in_specs=[row, _const_spec((1, D_MODEL)), _const_spec((D_MODEL, D_FF)),
                  _const_spec((D_MODEL, D_FF)), _const_spec((D_FF, D_MODEL))] + casts,
        out_specs=[row] + casts,
        compiler_params=pltpu.CompilerParams(
            dimension_semantics=("arbitrary",), vmem_limit_bytes=VMEM_LIMIT),
        name="ffn1",
    )(x, g, wg, wu, wd, *later_weights)
    return outs[0], outs[1:]


_DECAY_BLK = 256
_N_TERMS = 3
_BIAS_LANES = _N_TERMS * FOX_HEADS
_FF_LANES = 2 * _BIAS_LANES
_LOG2E = 1.4426950408889634
_NT = (((1,), (1,)), ((), ()))
_TN = (((0,), (0,)), ((), ()))
_N_CHUNK = TM_PROJ // CHUNK
_FOX_PIECE = 256
_EARLY_FILLERS = 3
_LATE_FILLERS = 1
_DK_SHIFT = GLA_DK.bit_length() - 1
_DV_SHIFT = GLA_DV.bit_length() - 1


def _decay_operands(small, bias, offset):
    lane = lax.broadcasted_iota(jnp.int32, (_DECAY_BLK, SMALL_W), 1)
    term = (lane >> 3) % _N_TERMS
    lane_all = lax.broadcasted_iota(jnp.int32, small.shape, 1)
    log_f = jnp.where(lane_all < _FF_LANES, _log_sigmoid(small + bias), 0.0)
    local = _segment_cumsum(log_f, _DECAY_BLK)
    ka, qa = [], []
    for b0 in range(0, small.shape[0], _DECAY_BLK):
        cum = local[b0:b0 + _DECAY_BLK] + offset
        offset = cum[_DECAY_BLK - 1:_DECAY_BLK, :]
        hi, mid, lo = _split3(cum * _LOG2E)
        f_term = jnp.where(term == 0, hi.astype(F32),
                           jnp.where(term == 1, mid.astype(F32), lo.astype(F32)))
        one = jnp.ones_like(f_term)
        zero = jnp.zeros_like(f_term)
        ka.append(jnp.where(lane < _BIAS_LANES, -f_term,
                            jnp.where(lane < _FF_LANES, one, zero)).astype(BF16))
        qa.append(jnp.where(lane < _BIAS_LANES, one,
                            jnp.where(lane < _FF_LANES, f_term, zero)).astype(BF16))
    return jnp.concatenate(ka, axis=0), jnp.concatenate(qa, axis=0), offset


def _gla_tile(gq, gk, gv, go, small, w2, b2, g_norm, state, o_ref, fillers):
    fillers = list(fillers)
    z = jnp.dot(small.astype(BF16), w2, preferred_element_type=F32) + b2
    for _ in range(min(_EARLY_FILLERS, len(fillers))):
        fillers.pop(0)()
    every = max(1, (2 * _N_CHUNK) // max(1, len(fillers)))

    def maybe_fill(step):
        if fillers and step % every == every - 1:
            fillers.pop(0)()

    log_a = _log_sigmoid(z) / GLA_TEMP

    g_cum_all = _segment_cumsum(log_a, CHUNK)
    chunks = [slice(n * CHUNK, (n + 1) * CHUNK) for n in range(_N_CHUNK)]
    bd_mask = ((lax.broadcasted_iota(jnp.int32, (GLA_KW, GLA_VW), 0) >> _DK_SHIFT)
               == (lax.broadcasted_iota(jnp.int32, (GLA_KW, GLA_VW), 1) >> _DV_SHIFT))

    g_tot_rows = jnp.concatenate(
        [g_cum_all[rows.stop - 1:rows.stop, :] for rows in chunks]
        + [jnp.zeros((LANES - _N_CHUNK, GLA_KW), F32)], axis=0)
    decay_cols = jnp.exp(g_tot_rows.T)

    deltas = []
    for n, rows in enumerate(chunks):
        maybe_fill(n)
        g_cum = g_cum_all[rows]
        k_dec = gk[rows] * jnp.exp(g_cum[CHUNK - 1:CHUNK, :] - g_cum)
        deltas.append(jnp.concatenate(
            [lax.dot_general(k_dec[:, h * GLA_DK:(h + 1) * GLA_DK].astype(BF16),
                             gv[rows, h * GLA_DV:(h + 1) * GLA_DV], _TN,
                             preferred_element_type=F32)
             for h in range(GLA_HEADS)], axis=0))

    states = []
    for n, delta in enumerate(deltas):
        state = decay_cols[:, n:n + 1] * state + delta
        states.append(state.astype(BF16))

    for n, (rows, st) in enumerate(zip(chunks, states)):
        maybe_fill(_N_CHUNK + n)
        st_bd = jnp.where(bd_mask, jnp.concatenate([st] * GLA_HEADS, axis=1), jnp.zeros((), BF16))
        o_all = jnp.dot(gq[rows], st_bd, preferred_element_type=F32)
        for h in range(GLA_HEADS):
            o = _rms(o_all[:, h * GLA_DV:(h + 1) * GLA_DV], g_norm)
            gate = go[rows, h * GLA_DV:(h + 1) * GLA_DV]
            o_ref[0, rows, h * GLA_DV:(h + 1) * GLA_DV] = (
                o * (gate * jax.nn.sigmoid(gate))).astype(o_ref.dtype)
    for fill in fillers:
        fill()
    return state


def _pack_w_in(wt_ref, w_ref):
    def put(c0, rows, scale=None):
        blk = wt_ref[rows, :]
        if scale is not None:
            blk = blk * scale
        w_ref[:, c0:c0 + blk.shape[0]] = blk.T.astype(BF16)

    def src(i):
        return slice(_IN_OFFS[i], _IN_OFFS[i + 1])

    put(_C_FQ, src(0), _LOG2E * FOX_DH ** -0.5)
    put(_C_FK, src(1))
    put(_C_FV, src(2))
    put(_C_GQ, src(4), GLA_DK ** -0.5)
    put(_C_GK, src(5))
    put(_C_GV, src(6))
    put(_C_GO, src(8))
    small = jnp.concatenate(
        [wt_ref[src(3), :]] * (_FF_LANES // FOX_HEADS) + [wt_ref[src(7), :]]
        + [jnp.zeros((SMALL_W - _FF_LANES - GLA_RANK, D_MODEL), F32)], axis=0)
    w_ref[:, _C_SM:] = small.T.astype(BF16)


def _mix_in_kernel(x_ref, g_ref, wt_ref, fb_ref, w2_ref, b2_ref, gn_ref,
                   fq_ref, fk_ref, fv_ref, ka_ref, qa_ref, gla_ref, st_ref, off_ref, w_ref):
    @pl.when((pl.program_id(0) == 0) & (pl.program_id(1) == 0))
    def _():
        _pack_w_in(wt_ref, w_ref)

    @pl.when(pl.program_id(1) == 0)
    def _():
        st_ref[...] = jnp.zeros_like(st_ref)
        off_ref[...] = jnp.zeros_like(off_ref)

    h = _rms_blocked(x_ref[0], g_ref[...])

    def proj(c0, width):
        return jnp.dot(h, w_ref[:, c0:c0 + width], preferred_element_type=F32)

    small = proj(_C_SM, SMALL_W)
    gk = proj(_C_GK, GLA_KW)
    gv = proj(_C_GV, GLA_VW).astype(BF16)
    gq = proj(_C_GQ, GLA_KW).astype(BF16)
    go = proj(_C_GO, GLA_VW)

    def fox_proj(out_ref, c0, col):
        def run():
            out_ref[0, :, col:col + _FOX_PIECE] = proj(c0 + col, _FOX_PIECE).astype(BF16)
        return run

    fillers = [fox_proj(ref, c0, col)
               for ref, c0 in ((fq_ref, _C_FQ), (fk_ref, _C_FK), (fv_ref, _C_FV))
               for col in range(0, FOX_WIDTH, _FOX_PIECE)]
    st_ref[...] = _gla_tile(gq, gk, gv, go, small, w2_ref[...], b2_ref[...], gn_ref[...],
                            st_ref[...], gla_ref, fillers[:-_LATE_FILLERS])
    for fill in fillers[-_LATE_FILLERS:]:
        fill()
    ka_ref[0], qa_ref[0], off_ref[...] = _decay_operands(small, fb_ref[...], off_ref[...])


def _mix_in(x1, g, w_t, fox_bias, w2, b2, gn):
    b, s, _ = x1.shape
    assert w_t.shape == (_IN_OFFS[-1], D_MODEL)

    def rows(width):
        return pl.BlockSpec((1, TM_PROJ, width), lambda i, t: (i, t, 0))

    widths = (FOX_WIDTH, FOX_WIDTH, FOX_WIDTH, SMALL_W, SMALL_W, GLA_VW)
    return pl.pallas_call(
        _mix_in_kernel,
        out_shape=[jax.ShapeDtypeStruct((b, s, wd), BF16) for wd in widths],
        grid=(b, s // TM_PROJ),
        in_specs=[rows(D_MODEL), _const_spec((1, D_MODEL)), _const_spec(w_t.shape),
                  _const_spec((1, SMALL_W)), _const_spec((SMALL_W, GLA_KW)),
                  _const_spec((1, GLA_KW)), _const_spec((1, GLA_DV))],
        out_specs=[rows(wd) for wd in widths],
        scratch_shapes=[pltpu.VMEM((GLA_KW, GLA_DV), F32), pltpu.VMEM((1, SMALL_W), F32),
                        pltpu.VMEM((D_MODEL, PROJ_W), BF16)],
        compiler_params=pltpu.CompilerParams(
            dimension_semantics=("arbitrary", "arbitrary"), vmem_limit_bytes=VMEM_LIMIT),
        name="mix_in",
    )(x1, g, w_t, fox_bias, w2, b2, gn)


_LOOKAHEAD = 3
PAIRS_PER_STEP = 2

def _attn_kernel(q_ref, k_ref, v_ref, qa_ref, ka_ref, o_ref):
    seq = q_ref.shape[1]
    lane = lax.broadcasted_iota(jnp.int32, (TQ, LANES), 1)
    k_idx = lax.broadcasted_iota(jnp.int32, (TQ, TQ), 0)
    q_idx = lax.broadcasted_iota(jnp.int32, (TQ, TQ), 1)
    causal = k_idx <= q_idx
    pair_lanes = [slice(p * LANES, (p + 1) * LANES) for p in range(PAIRS_PER_STEP)]

    k_aug = [jnp.concatenate([k_ref[0, :, cols], ka_ref[0]], axis=1) for cols in pair_lanes]
    v_t = [v_ref[0, :, cols].T for cols in pair_lanes]

    def logits(p, qi, hh):
        q0 = qi * TQ
        klen = q0 + TQ
        q = q_ref[0, q0:klen, pair_lanes[p]]
        qa = qa_ref[0, q0:klen, :]
        head = 2 * (pl.program_id(1) * PAIRS_PER_STEP + p) + hh
        in_head = (lane >= hh * FOX_DH) & (lane < (hh + 1) * FOX_DH)
        bias_lane = ((lane & (FOX_HEADS - 1)) == head) & (lane < _FF_LANES)
        q_aug = jnp.concatenate([jnp.where(in_head, q, jnp.zeros_like(q)),
                                 jnp.where(bias_lane, qa, jnp.zeros_like(qa))], axis=1)
        return lax.dot_general(k_aug[p][:klen], q_aug, _NT, preferred_element_type=F32)

    done = {}

    def finish(p, qi, hh, s_t):
        q0 = qi * TQ
        klen = q0 + TQ
        s_diag = jnp.where(causal, s_t[q0:], -jnp.inf)
        m = jnp.max(s_diag, axis=0, keepdims=True)
        if qi > 0:
            m = jnp.maximum(m, jnp.max(s_t[:q0], axis=0, keepdims=True))
        o_t, l = None, None
        for k0 in range(0, klen, TQ):
            p_t = jnp.exp2((s_diag if k0 == q0 else s_t[k0:k0 + TQ]) - m)
            l_t = jnp.sum(p_t, axis=0, keepdims=True)
            part = jnp.dot(v_t[p][hh * FOX_DH:(hh + 1) * FOX_DH, k0:k0 + TQ],
                           p_t.astype(BF16), preferred_element_type=F32)
            o_t = part if o_t is None else o_t + part
            l = l_t if l is None else l + l_t
        done[(p, qi, hh)] = o_t * (1.0 / l)
        if (p, qi, 1 - hh) in done:
            outs = [done.pop((p, qi, h)) for h in range(2)]
            o_ref[0, q0:klen, pair_lanes[p]] = jnp.concatenate(outs, axis=0).T.astype(o_ref.dtype)

    n_blocks = seq // TQ
    order = list(range(1, n_blocks, 2)) + list(range(n_blocks - 2 + n_blocks % 2, -1, -2))
    items = [(p, qi, hh) for qi in order for hh in range(2) for p in range(PAIRS_PER_STEP)]
    ready = {item: logits(*item) for item in items[:_LOOKAHEAD]}
    for pos, item in enumerate(items):
        if pos + _LOOKAHEAD < len(items):
            ahead = items[pos + _LOOKAHEAD]
            ready[ahead] = logits(*ahead)
        finish(*item, ready.pop(item))


def _fox_attn(fq, fk, fv, qa, ka):
    b, s, _ = fq.shape
    qkv = pl.BlockSpec((1, s, PAIRS_PER_STEP * LANES), lambda i, h: (i, 0, h))
    aug = pl.BlockSpec((1, s, SMALL_W), lambda i, h: (i, 0, 0))
    return pl.pallas_call(
        _attn_kernel,
        out_shape=jax.ShapeDtypeStruct((b, s, FOX_WIDTH), BF16),
        grid=(b, FOX_HEADS // (2 * PAIRS_PER_STEP)),
        in_specs=[qkv, qkv, qkv, aug, aug],
        out_specs=qkv,
        compiler_params=pltpu.CompilerParams(
            dimension_semantics=("arbitrary", "arbitrary"), vmem_limit_bytes=VMEM_LIMIT),
        name="fox_attn",
    )(fq, fk, fv, qa, ka)


def _out_kernel(x_ref, fox_ref, gla_ref, wo_ref, g_ref, wg_ref, wu_ref, wd_ref, gf_ref, o_ref):
    x2, h = [], []
    for rows in _row_blocks(x_ref.shape[0], NORM_BLOCKS):
        mix = (jnp.dot(fox_ref[rows, :], wo_ref[:FOX_WIDTH, :], preferred_element_type=F32)
               + jnp.dot(gla_ref[rows, :], wo_ref[FOX_WIDTH:, :], preferred_element_type=F32))
        x2.append(x_ref[rows, :] + mix)
        h.append(_rms(x2[-1], g_ref[...]).astype(BF16))
    x2 = jnp.concatenate(x2, axis=0)
    for rows, y in _swiglu(jnp.concatenate(h, axis=0), wg_ref, wu_ref, wd_ref):
        o_ref[rows, :] = _rms(x2[rows] + 0.5 * y, gf_ref[...])


def _out_ffn2(x1, fox, gla, wo, g, wg, wu, wd, gf):
    n = x1.shape[0]

    def rows(width):
        return pl.BlockSpec((TM_FFN, width), lambda i: (i, 0))

    return pl.pallas_call(
        _out_kernel,
        out_shape=jax.ShapeDtypeStruct((n, D_MODEL), F32),
        grid=(n // TM_FFN,),
        in_specs=[rows(D_MODEL), rows(FOX_WIDTH), rows(GLA_VW),
                  _const_spec((FOX_WIDTH + GLA_VW, D_MODEL)), _const_spec((1, D_MODEL)),
                  _const_spec((D_MODEL, D_FF)), _const_spec((D_MODEL, D_FF)),
                  _const_spec((D_FF, D_MODEL)), _const_spec((1, D_MODEL))],
        out_specs=rows(D_MODEL),
        compiler_params=pltpu.CompilerParams(
            dimension_semantics=("arbitrary",), vmem_limit_bytes=VMEM_LIMIT),
        name="out_ffn2",
    )(x1, fox, gla, wo, g, wg, wu, wd, gf)


def kernel(x, ffn1_norm, ffn1_w_gate, ffn1_w_up, ffn1_w_down, mix_norm, w_in, fox_forget_bias,
           gla_w_gate_up, gla_gate_bias, gla_out_norm, w_out, ffn2_norm, ffn2_w_gate, ffn2_w_up,
           ffn2_w_down, final_norm):
    b, s, d = x.shape
    n = b * s
    assert ffn1_norm.shape[0] == 1, "the final norm is fused into the single layer's last call"
    x = x.reshape(n, d)
    for l in range(1):
        x1, (wo, wg2, wu2, wd2) = _ffn1(
            x, ffn1_norm[l].reshape(1, d), ffn1_w_gate[l].astype(BF16),
            ffn1_w_up[l].astype(BF16), ffn1_w_down[l].astype(BF16),
            [w_out[l], ffn2_w_gate[l], ffn2_w_up[l], ffn2_w_down[l]])

        bias = jnp.zeros((1, SMALL_W), F32).at[0, :_FF_LANES].set(
            jnp.tile(fox_forget_bias[l], _FF_LANES // FOX_HEADS))
        w2 = jnp.zeros((SMALL_W, GLA_KW), F32).at[_FF_LANES:_FF_LANES + GLA_RANK].set(
            gla_w_gate_up[l]).astype(BF16)
        fq, fk, fv, ka, qa, gla = _mix_in(
            x1.reshape(b, s, d), mix_norm[l].reshape(1, d), w_in[l].T, bias, w2,
            gla_gate_bias[l].reshape(1, GLA_KW), gla_out_norm[l].reshape(1, GLA_DV))
        fox = _fox_attn(fq, fk, fv, qa, ka)

        x = _out_ffn2(x1, fox.reshape(n, -1), gla.reshape(n, -1), wo,
                      ffn2_norm[l].reshape(1, d), wg2, wu2, wd2,
                      final_norm.reshape(1, d))
    return x.reshape(b, s, d)
```

```python
import jax
import jax.numpy as jnp
from jax import lax
from jax.experimental import pallas as pl
from jax.experimental.pallas import tpu as pltpu

F32 = jnp.float32
BF16 = jnp.bfloat16

D_MODEL = 1024
D_FF = 2816
EPS = 1e-6
CHUNK = 64

FOX_HEADS = 8
FOX_DH = 64
FOX_WIDTH = FOX_HEADS * FOX_DH
GLA_HEADS = 4
GLA_DK = 64
GLA_DV = 128
GLA_KW = GLA_HEADS * GLA_DK
GLA_VW = GLA_HEADS * GLA_DV
GLA_RANK = 16
GLA_TEMP = 16.0

LANES = 128
BF16_SUBLANES = 16
SMALL_W = LANES
_C_FQ = 0
_C_FK = _C_FQ + FOX_WIDTH
_C_FV = _C_FK + FOX_WIDTH
_C_GQ = _C_FV + FOX_WIDTH
_C_GK = _C_GQ + GLA_KW
_C_GV = _C_GK + GLA_KW
_C_GO = _C_GV + GLA_VW
_C_SM = _C_GO + GLA_VW
PROJ_W = _C_SM + SMALL_W
_IN_OFFS = (0,)
for _w in (FOX_WIDTH, FOX_WIDTH, FOX_WIDTH, FOX_HEADS, GLA_KW, GLA_KW, GLA_VW, GLA_RANK, GLA_VW):
    _IN_OFFS += (_IN_OFFS[-1] + _w,)

TM_FFN = 1024
FF_BOUNDS = (0, 1024, 2048, D_FF)
NORM_BLOCKS = 4
TAIL_BLOCKS = 2
TM_PROJ = 1024
TQ = 256
VMEM_LIMIT = 56 * 1024 * 1024


def _rms(x, g):
    return x * lax.rsqrt(jnp.mean(x * x, axis=-1, keepdims=True) + EPS) * g


def _log_sigmoid(z):
    return jnp.minimum(z, 0.0) - jnp.log(1.0 + jnp.exp(-jnp.abs(z)))


def _row_blocks(n_rows, n_blocks):
    step = n_rows // n_blocks
    return [slice(r0, r0 + step) for r0 in range(0, n_rows, step)]


def _rms_blocked(x, g):
    return jnp.concatenate(
        [_rms(x[rows], g).astype(BF16) for rows in _row_blocks(x.shape[0], NORM_BLOCKS)], axis=0)


def _swiglu(h, wg_ref, wu_ref, wd_ref):
    y = None
    chunks = list(zip(FF_BOUNDS[:-1], FF_BOUNDS[1:]))
    for c0, c1 in chunks:
        gate = jnp.dot(h, wg_ref[:, c0:c1], preferred_element_type=F32)
        up = jnp.dot(h, wu_ref[:, c0:c1], preferred_element_type=F32)
        act = (gate * jax.nn.sigmoid(gate) * up).astype(BF16)
        if (c0, c1) != chunks[-1]:
            part = jnp.dot(act, wd_ref[c0:c1, :], preferred_element_type=F32)
            y = part if y is None else y + part
    out = []
    for rows in _row_blocks(h.shape[0], TAIL_BLOCKS):
        part = jnp.dot(act[rows], wd_ref[c0:c1, :], preferred_element_type=F32)
        out.append((rows, part if y is None else y[rows] + part))
    return out


def _segment_cumsum(x, seg):
    row = lax.broadcasted_iota(jnp.int32, x.shape, 0) & (seg - 1)
    shift = 1
    while shift < seg:
        x = x + jnp.where(row >= shift, pltpu.roll(x, shift, 0), 0.0)
        shift *= 2
    return x


def _split3(x):
    hi = x.astype(BF16)
    r1 = x - hi.astype(F32)
    mid = r1.astype(BF16)
    lo = (r1 - mid.astype(F32)).astype(BF16)
    return hi, mid, lo


def _const_spec(shape):
    n = len(shape)
    return pl.BlockSpec(shape, lambda *_: (0,) * n, pipeline_mode=pl.Buffered(1))


def _ffn1_kernel(x_ref, g_ref, wg_ref, wu_ref, wd_ref, *rest):
    n_cast = (len(rest) - 1) // 2
    o_ref = rest[n_cast]
    for src, dst in zip(rest[:n_cast], rest[n_cast + 1:]):
        dst[...] = src[...].astype(BF16)
    h = _rms_blocked(x_ref[...], g_ref[...])
    for rows, y in _swiglu(h, wg_ref, wu_ref, wd_ref):
        o_ref[rows, :] = x_ref[rows, :] + 0.5 * y


def _ffn1(x, g, wg, wu, wd, later_weights):
    n = x.shape[0]
    steps = n // TM_FFN
    row = pl.BlockSpec((TM_FFN, D_MODEL), lambda i: (i, 0))

    def cast_spec(w):
        hold = next(k for k in (1, 2, 4, 8) if (w.shape[0] * k) % (steps * BF16_SUBLANES) == 0)
        return pl.BlockSpec((w.shape[0] * hold // steps, w.shape[1]), lambda i: (i // hold, 0))

    casts = [cast_spec(w) for w in later_weights]
    outs = pl.pallas_call(
        _ffn1_kernel,
        out_shape=[jax.ShapeDtypeStruct((n, D_MODEL), F32)]
        + [jax.ShapeDtypeStruct(w.shape, BF16) for w in later_weights],
        grid=(steps,),
        in_specs=[row, _const_spec((1, D_MODEL)), _const_spec((D_MODEL, D_FF)),
                  _const_spec((D_MODEL, D_FF)), _const_spec((D_FF, D_MODEL))] + casts,
        out_specs=[row] + casts,
        compiler_params=pltpu.CompilerParams(
            dimension_semantics=("arbitrary",), vmem_limit_bytes=VMEM_LIMIT),
        name="ffn1",
    )(x, g, wg, wu, wd, *later_weights)
    return outs[0], outs[1:]


_DECAY_BLK = 256
_N_TERMS = 3
_BIAS_LANES = _N_TERMS * FOX_HEADS
_FF_LANES = 2 * _BIAS_LANES
_LOG2E = 1.4426950408889634
_NT = (((1,), (1,)), ((), ()))
_TN = (((0,), (0,)), ((), ()))
_N_CHUNK = TM_PROJ // CHUNK
_FOX_PIECE = 256
_EARLY_FILLERS = 3
_LATE_FILLERS = 1
_DK_SHIFT = GLA_DK.bit_length() - 1
_DV_SHIFT = GLA_DV.bit_length() - 1


def _decay_operands(small, bias, offset):
    lane = lax.broadcasted_iota(jnp.int32, (_DECAY_BLK, SMALL_W), 1)
    term = (lane >> 3) % _N_TERMS
    lane_all = lax.broadcasted_iota(jnp.int32, small.shape, 1)
    log_f = jnp.where(lane_all < _FF_LANES, _log_sigmoid(small + bias), 0.0)
    local = _segment_cumsum(log_f, _DECAY_BLK)
    ka, qa = [], []
    for b0 in range(0, small.shape[0], _DECAY_BLK):
        cum = local[b0:b0 + _DECAY_BLK] + offset
        offset = cum[_DECAY_BLK - 1:_DECAY_BLK, :]
        hi, mid, lo = _split3(cum * _LOG2E)
        f_term = jnp.where(term == 0, hi.astype(F32),
                           jnp.where(term == 1, mid.astype(F32), lo.astype(F32)))
        one = jnp.ones_like(f_term)
        zero = jnp.zeros_like(f_term)
        ka.append(jnp.where(lane < _BIAS_LANES, -f_term,
                            jnp.where(lane < _FF_LANES, one, zero)).astype(BF16))
        qa.append(jnp.where(lane < _BIAS_LANES, one,
                            jnp.where(lane < _FF_LANES, f_term, zero)).astype(BF16))
    return jnp.concatenate(ka, axis=0), jnp.concatenate(qa, axis=0), offset


def _gla_tile(gq, gk, gv, go, small, w2, b2, g_norm, state, o_ref, fillers):
    fillers = list(fillers)
    z = jnp.dot(small.astype(BF16), w2, preferred_element_type=F32) + b2
    for _ in range(min(_EARLY_FILLERS, len(fillers))):
        fillers.pop(0)()
    every = max(1, (2 * _N_CHUNK) // max(1, len(fillers)))

    def maybe_fill(step):
        if fillers and step % every == every - 1:
            fillers.pop(0)()

    log_a = _log_sigmoid(z) / GLA_TEMP

    g_cum_all = _segment_cumsum(log_a, CHUNK)
    chunks = [slice(n * CHUNK, (n + 1) * CHUNK) for n in range(_N_CHUNK)]
    bd_mask = ((lax.broadcasted_iota(jnp.int32, (GLA_KW, GLA_VW), 0) >> _DK_SHIFT)
               == (lax.broadcasted_iota(jnp.int32, (GLA_KW, GLA_VW), 1) >> _DV_SHIFT))

    g_tot_rows = jnp.concatenate(
        [g_cum_all[rows.stop - 1:rows.stop, :] for rows in chunks]
        + [jnp.zeros((LANES - _N_CHUNK, GLA_KW), F32)], axis=0)
    decay_cols = jnp.exp(g_tot_rows.T)

    deltas = []
    for n, rows in enumerate(chunks):
        maybe_fill(n)
        g_cum = g_cum_all[rows]
        k_dec = gk[rows] * jnp.exp(g_cum[CHUNK - 1:CHUNK, :] - g_cum)
        deltas.append(jnp.concatenate(
            [lax.dot_general(k_dec[:, h * GLA_DK:(h + 1) * GLA_DK].astype(BF16),
                             gv[rows, h * GLA_DV:(h + 1) * GLA_DV], _TN,
                             preferred_element_type=F32)
             for h in range(GLA_HEADS)], axis=0))

    states = []
    for n, delta in enumerate(deltas):
        state = decay_cols[:, n:n + 1] * state + delta
        states.append(state.astype(BF16))

    for n, (rows, st) in enumerate(zip(chunks, states)):
        maybe_fill(_N_CHUNK + n)
        st_bd = jnp.where(bd_mask, jnp.concatenate([st] * GLA_HEADS, axis=1), jnp.zeros((), BF16))
        o_all = jnp.dot(gq[rows], st_bd, preferred_element_type=F32)
        for h in range(GLA_HEADS):
            o = _rms(o_all[:, h * GLA_DV:(h + 1) * GLA_DV], g_norm)
            gate = go[rows, h * GLA_DV:(h + 1) * GLA_DV]
            o_ref[0, rows, h * GLA_DV:(h + 1) * GLA_DV] = (
                o * (gate * jax.nn.sigmoid(gate))).astype(o_ref.dtype)
    for fill in fillers:
        fill()
    return state


def _pack_w_in(wt_ref, w_ref):
    def put(c0, rows, scale=None):
        blk = wt_ref[rows, :]
        if scale is not None:
            blk = blk * scale
        w_ref[:, c0:c0 + blk.shape[0]] = blk.T.astype(BF16)

    def src(i):
        return slice(_IN_OFFS[i], _IN_OFFS[i + 1])

    put(_C_FQ, src(0), _LOG2E * FOX_DH ** -0.5)
    put(_C_FK, src(1))
    put(_C_FV, src(2))
    put(_C_GQ, src(4), GLA_DK ** -0.5)
    put(_C_GK, src(5))
    put(_C_GV, src(6))
    put(_C_GO, src(8))
    small = jnp.concatenate(
        [wt_ref[src(3), :]] * (_FF_LANES // FOX_HEADS) + [wt_ref[src(7), :]]
        + [jnp.zeros((SMALL_W - _FF_LANES - GLA_RANK, D_MODEL), F32)], axis=0)
    w_ref[:, _C_SM:] = small.T.astype(BF16)


def _mix_in_kernel(x_ref, g_ref, wt_ref, fb_ref, w2_ref, b2_ref, gn_ref,
                   fq_ref, fk_ref, fv_ref, ka_ref, qa_ref, gla_ref, st_ref, off_ref, w_ref):
    @pl.when((pl.program_id(0) == 0) & (pl.program_id(1) == 0))
    def _():
        _pack_w_in(wt_ref, w_ref)

    @pl.when(pl.program_id(1) == 0)
    def _():
        st_ref[...] = jnp.zeros_like(st_ref)
        off_ref[...] = jnp.zeros_like(off_ref)

    h = _rms_blocked(x_ref[0], g_ref[...])

    def proj(c0, width):
        return jnp.dot(h, w_ref[:, c0:c0 + width], preferred_element_type=F32)

    small = proj(_C_SM, SMALL_W)
    gk = proj(_C_GK, GLA_KW)
    gv = proj(_C_GV, GLA_VW).astype(BF16)
    gq = proj(_C_GQ, GLA_KW).astype(BF16)
    go = proj(_C_GO, GLA_VW)

    def fox_proj(out_ref, c0, col):
        def run():
            out_ref[0, :, col:col + _FOX_PIECE] = proj(c0 + col, _FOX_PIECE).astype(BF16)
        return run

    fillers = [fox_proj(ref, c0, col)
               for ref, c0 in ((fq_ref, _C_FQ), (fk_ref, _C_FK), (fv_ref, _C_FV))
               for col in range(0, FOX_WIDTH, _FOX_PIECE)]
    st_ref[...] = _gla_tile(gq, gk, gv, go, small, w2_ref[...], b2_ref[...], gn_ref[...],
                            st_ref[...], gla_ref, fillers[:-_LATE_FILLERS])
    for fill in fillers[-_LATE_FILLERS:]:
        fill()
    ka_ref[0], qa_ref[0], off_ref[...] = _decay_operands(small, fb_ref[...], off_ref[...])


def _mix_in(x1, g, w_t, fox_bias, w2, b2, gn):
    b, s, _ = x1.shape
    assert w_t.shape == (_IN_OFFS[-1], D_MODEL)

    def rows(width):
        return pl.BlockSpec((1, TM_PROJ, width), lambda i, t: (i, t, 0))

    widths = (FOX_WIDTH, FOX_WIDTH, FOX_WIDTH, SMALL_W, SMALL_W, GLA_VW)
    return pl.pallas_call(
        _mix_in_kernel,
        out_shape=[jax.ShapeDtypeStruct((b, s, wd), BF16) for wd in widths],
        grid=(b, s // TM_PROJ),
        in_specs=[rows(D_MODEL), _const_spec((1, D_MODEL)), _const_spec(w_t.shape),
                  _const_spec((1, SMALL_W)), _const_spec((SMALL_W, GLA_KW)),
                  _const_spec((1, GLA_KW)), _const_spec((1, GLA_DV))],
        out_specs=[rows(wd) for wd in widths],
        scratch_shapes=[pltpu.VMEM((GLA_KW, GLA_DV), F32), pltpu.VMEM((1, SMALL_W), F32),
                        pltpu.VMEM((D_MODEL, PROJ_W), BF16)],
        compiler_params=pltpu.CompilerParams(
            dimension_semantics=("arbitrary", "arbitrary"), vmem_limit_bytes=VMEM_LIMIT),
        name="mix_in",
    )(x1, g, w_t, fox_bias, w2, b2, gn)


_LOOKAHEAD = 3
PAIRS_PER_STEP = 2

def _attn_kernel(q_ref, k_ref, v_ref, qa_ref, ka_ref, o_ref):
    seq = q_ref.shape[1]
    lane = lax.broadcasted_iota(jnp.int32, (TQ, LANES), 1)
    half = TQ // 2
    k_idx = lax.broadcasted_iota(jnp.int32, (half, half), 0)
    q_idx = lax.broadcasted_iota(jnp.int32, (half, half), 1)
    causal = k_idx <= q_idx
    pair_lanes = [slice(p * LANES, (p + 1) * LANES) for p in range(PAIRS_PER_STEP)]

    k_aug = [jnp.concatenate([k_ref[0, :, cols], ka_ref[0]], axis=1) for cols in pair_lanes]
    v_t = [v_ref[0, :, cols].T for cols in pair_lanes]

    def logits(p, qi, hh):
        q0 = qi * TQ
        klen = q0 + TQ
        q = q_ref[0, q0:klen, pair_lanes[p]]
        qa = qa_ref[0, q0:klen, :]
        head = 2 * (pl.program_id(1) * PAIRS_PER_STEP + p) + hh
        in_head = (lane >= hh * FOX_DH) & (lane < (hh + 1) * FOX_DH)
        bias_lane = ((lane & (FOX_HEADS - 1)) == head) & (lane < _FF_LANES)
        q_aug = jnp.concatenate([jnp.where(in_head, q, jnp.zeros_like(q)),
                                 jnp.where(bias_lane, qa, jnp.zeros_like(qa))], axis=1)
        return lax.dot_general(k_aug[p][:klen], q_aug, _NT, preferred_element_type=F32)

    done = {}

    def finish(p, qi, hh, s_t):
        q0 = qi * TQ
        klen = q0 + TQ
        top = s_t[q0:q0 + half]
        top = jnp.concatenate([jnp.where(causal, top[:, :half], -jnp.inf), top[:, half:]], axis=1)
        bot = jnp.where(causal, s_t[q0 + half:klen, half:], -jnp.inf)
        m = jnp.max(top, axis=0, keepdims=True)
        m = jnp.concatenate(
            [m[:, :half], jnp.maximum(m[:, half:], jnp.max(bot, axis=0, keepdims=True))], axis=1)
        if qi > 0:
            m = jnp.maximum(m, jnp.max(s_t[:q0], axis=0, keepdims=True))
        o_t, l = None, None
        for k0 in range(0, klen, TQ):
            if k0 == q0:
                p_top = jnp.exp2(top - m)
                p_bot = jnp.exp2(bot - m[:, half:])
                l_t = jnp.sum(p_top, axis=0, keepdims=True) + jnp.concatenate(
                    [jnp.zeros((1, half), F32), jnp.sum(p_bot, axis=0, keepdims=True)], axis=1)
                p_t = jnp.concatenate(
                    [p_top, jnp.concatenate([jnp.zeros_like(p_bot), p_bot], axis=1)], axis=0)
            else:
                p_t = jnp.exp2(s_t[k0:k0 + TQ] - m)
                l_t = jnp.sum(p_t, axis=0, keepdims=True)
            part = jnp.dot(v_t[p][hh * FOX_DH:(hh + 1) * FOX_DH, k0:k0 + TQ],
                           p_t.astype(BF16), preferred_element_type=F32)
            o_t = part if o_t is None else o_t + part
            l = l_t if l is None else l + l_t
        done[(p, qi, hh)] = o_t * (1.0 / l)
        if (p, qi, 1 - hh) in done:
            outs = [done.pop((p, qi, h)) for h in range(2)]
            o_ref[0, q0:klen, pair_lanes[p]] = jnp.concatenate(outs, axis=0).T.astype(o_ref.dtype)

    n_blocks = seq // TQ
    order = list(range(1, n_blocks, 2)) + list(range(n_blocks - 2 + n_blocks % 2, -1, -2))
    items = [(p, qi, hh) for qi in order for hh in range(2) for p in range(PAIRS_PER_STEP)]
    ready = {item: logits(*item) for item in items[:_LOOKAHEAD]}
    for pos, item in enumerate(items):
        if pos + _LOOKAHEAD < len(items):
            ahead = items[pos + _LOOKAHEAD]
            ready[ahead] = logits(*ahead)
        finish(*item, ready.pop(item))


def _fox_attn(fq, fk, fv, qa, ka):
    b, s, _ = fq.shape
    qkv = pl.BlockSpec((1, s, PAIRS_PER_STEP * LANES), lambda i, h: (i, 0, h))
    aug = pl.BlockSpec((1, s, SMALL_W), lambda i, h: (i, 0, 0))
    return pl.pallas_call(
        _attn_kernel,
        out_shape=jax.ShapeDtypeStruct((b, s, FOX_WIDTH), BF16),
        grid=(b, FOX_HEADS // (2 * PAIRS_PER_STEP)),
        in_specs=[qkv, qkv, qkv, aug, aug],
        out_specs=qkv,
        compiler_params=pltpu.CompilerParams(
            dimension_semantics=("arbitrary", "arbitrary"), vmem_limit_bytes=VMEM_LIMIT),
        name="fox_attn",
    )(fq, fk, fv, qa, ka)


def _out_kernel(x_ref, fox_ref, gla_ref, wo_ref, g_ref, wg_ref, wu_ref, wd_ref, gf_ref, o_ref):
    x2, h = [], []
    for rows in _row_blocks(x_ref.shape[0], NORM_BLOCKS):
        mix = (jnp.dot(fox_ref[rows, :], wo_ref[:FOX_WIDTH, :], preferred_element_type=F32)
               + jnp.dot(gla_ref[rows, :], wo_ref[FOX_WIDTH:, :], preferred_element_type=F32))
        x2.append(x_ref[rows, :] + mix)
        h.append(_rms(x2[-1], g_ref[...]).astype(BF16))
    x2 = jnp.concatenate(x2, axis=0)
    for rows, y in _swiglu(jnp.concatenate(h, axis=0), wg_ref, wu_ref, wd_ref):
        o_ref[rows, :] = _rms(x2[rows] + 0.5 * y, gf_ref[...])


def _out_ffn2(x1, fox, gla, wo, g, wg, wu, wd, gf):
    n = x1.shape[0]

    def rows(width):
        return pl.BlockSpec((TM_FFN, width), lambda i: (i, 0))

    return pl.pallas_call(
        _out_kernel,
        out_shape=jax.ShapeDtypeStruct((n, D_MODEL), F32),
        grid=(n // TM_FFN,),
        in_specs=[rows(D_MODEL), rows(FOX_WIDTH), rows(GLA_VW),
                  _const_spec((FOX_WIDTH + GLA_VW, D_MODEL)), _const_spec((1, D_MODEL)),
                  _const_spec((D_MODEL, D_FF)), _const_spec((D_MODEL, D_FF)),
                  _const_spec((D_FF, D_MODEL)), _const_spec((1, D_MODEL))],
        out_specs=rows(D_MODEL),
        compiler_params=pltpu.CompilerParams(
            dimension_semantics=("arbitrary",), vmem_limit_bytes=VMEM_LIMIT),
        name="out_ffn2",
    )(x1, fox, gla, wo, g, wg, wu, wd, gf)


def kernel(x, ffn1_norm, ffn1_w_gate, ffn1_w_up, ffn1_w_down, mix_norm, w_in, fox_forget_bias,
           gla_w_gate_up, gla_gate_bias, gla_out_norm, w_out, ffn2_norm, ffn2_w_gate, ffn2_w_up,
           ffn2_w_down, final_norm):
    b, s, d = x.shape
    n = b * s
    assert ffn1_norm.shape[0] == 1, "the final norm is fused into the single layer's last call"
    x = x.reshape(n, d)
    for l in range(1):
        x1, (wo, wg2, wu2, wd2) = _ffn1(
            x, ffn1_norm[l].reshape(1, d), ffn1_w_gate[l].astype(BF16),
            ffn1_w_up[l].astype(BF16), ffn1_w_down[l].astype(BF16),
            [w_out[l], ffn2_w_gate[l], ffn2_w_up[l], ffn2_w_down[l]])

        bias = jnp.zeros((1, SMALL_W), F32).at[0, :_FF_LANES].set(
            jnp.tile(fox_forget_bias[l], _FF_LANES // FOX_HEADS))
        w2 = jnp.zeros((SMALL_W, GLA_KW), F32).at[_FF_LANES:_FF_LANES + GLA_RANK].set(
            gla_w_gate_up[l]).astype(BF16)
        fq, fk, fv, ka, qa, gla = _mix_in(
            x1.reshape(b, s, d), mix_norm[l].reshape(1, d), w_in[l].T, bias, w2,
            gla_gate_bias[l].reshape(1, GLA_KW), gla_out_norm[l].reshape(1, GLA_DV))
        fox = _fox_attn(fq, fk, fv, qa, ka)

        x = _out_ffn2(x1, fox.reshape(n, -1), gla.reshape(n, -1), wo,
                      ffn2_norm[l].reshape(1, d), wg2, wu2, wd2,
                      final_norm.reshape(1, d))
    return x.reshape(b, s, d)
```
